```python
import math
import jax, jax.numpy as jnp
from jax import lax
import numpy as np

D_MODEL = 1024
BATCH = 8
SEQ = 4096
DEPTH = 4

CHUNK = 64
N_MIXERS = 2
S5_GROUP = 16
S5_GROUPS = D_MODEL // S5_GROUP
S5_STATE = 64
S5_DT_MIN = 0.001
S5_DT_MAX = 0.1
S5_LAMBDA_RE_MAX = -1e-4
DA_HEADS = 8
DA_HEAD_DIM = D_MODEL // DA_HEADS // 2
DA_V_DIM = 2 * DA_HEAD_DIM
ROPE_THETA = 10000.0
Q_BLOCK = 128
MAX_POS_OFFSET = 65536
D_FF = 2816
CONV_WIDTH = 3
EPS = 1e-6

N_S5 = (DEPTH + 1) // 2
N_DA = DEPTH // 2

kernel_name = "hybrid_s5_diffattn_convffn_block"


def rms_norm(h, g):
    hf = h.astype(jnp.float32)
    hf = hf * lax.rsqrt(jnp.mean(hf * hf, axis=-1, keepdims=True) + EPS)
    return (hf * g.astype(jnp.float32)).astype(h.dtype)


def modulate(h, shift, scale):
    return h * (1.0 + scale[:, None, :]) + shift[:, None, :]


def rope(t, pos):
    half = t.shape[-1] // 2
    inv = ROPE_THETA ** (-jnp.arange(half, dtype=jnp.float32) / half)
    ang = pos.astype(jnp.float32)[..., None] * inv
    cos = jnp.cos(ang)[:, :, None, :]
    sin = jnp.sin(ang)[:, :, None, :]
    tf = t.astype(jnp.float32)
    t1, t2 = tf[..., :half], tf[..., half:]
    return jnp.concatenate([t1 * cos - t2 * sin, t2 * cos + t1 * sin], axis=-1).astype(t.dtype)


def s5_mixer(u, a_re, a_im, log_dt, b_re, b_im, c_re, c_im, d_skip, w_glu):
    bsz, seq_len, d = u.shape
    f32 = jnp.float32
    lam_re = jnp.minimum(a_re.astype(f32), S5_LAMBDA_RE_MAX)
    lam_im = a_im.astype(f32)
    dt = jnp.exp(log_dt.astype(f32))[:, None]
    dre, dimg = lam_re * dt, lam_im * dt
    mag = jnp.exp(dre)
    lb_re, lb_im = mag * jnp.cos(dimg), mag * jnp.sin(dimg)
    den = lam_re * lam_re + lam_im * lam_im
    nr = lb_re - 1.0
    f_re = (nr * lam_re + lb_im * lam_im) / den
    f_im = (lb_im * lam_re - nr * lam_im) / den
    br, bi = b_re.astype(f32), b_im.astype(f32)
    bb_re = f_re[..., None] * br - f_im[..., None] * bi
    bb_im = f_re[..., None] * bi + f_im[..., None] * br
    cr, ci = c_re.astype(f32), c_im.astype(f32)
    k = jnp.arange(1, CHUNK + 1, dtype=f32)[:, None, None]
    pmag = jnp.exp(k * dre)
    pw_re, pw_im = pmag * jnp.cos(k * dimg), pmag * jnp.sin(k * dimg)

    u32 = u.astype(f32)
    n_chunks = seq_len // CHUNK
    u_c = jnp.moveaxis(u32.reshape(bsz, n_chunks, CHUNK, S5_GROUPS, S5_GROUP), 1, 0)

    def combine(e1, e2):
        a1r, a1i, b1r, b1i = e1
        a2r, a2i, b2r, b2i = e2
        return (a1r * a2r - a1i * a2i, a1r * a2i + a1i * a2r,
                a2r * b1r - a2i * b1i + b2r, a2r * b1i + a2i * b1r + b2i)

    def chunk_step(carry, uc):
        h_re, h_im = carry
        bu_re = jnp.einsum("btgc,gpc->btgp", uc, bb_re)
        bu_im = jnp.einsum("btgc,gpc->btgp", uc, bb_im)
        ar = jnp.broadcast_to(lb_re, bu_re.shape)
        ai = jnp.broadcast_to(lb_im, bu_im.shape)
        _, _, hl_re, hl_im = lax.associative_scan(combine, (ar, ai, bu_re, bu_im), axis=1)
        hp_re, hp_im = h_re[:, None], h_im[:, None]
        ht_re = hl_re + pw_re * hp_re - pw_im * hp_im
        ht_im = hl_im + pw_re * hp_im + pw_im * hp_re
        y = (jnp.einsum("btgp,gcp->btgc", ht_re, cr)
             - jnp.einsum("btgp,gcp->btgc", ht_im, ci))
        return (ht_re[:, -1], ht_im[:, -1]), y

    h0 = jnp.zeros((bsz, S5_GROUPS, S5_STATE), f32)
    _, ys = lax.scan(chunk_step, (h0, h0), u_c)
    y = jnp.moveaxis(ys, 0, 1).reshape(bsz, seq_len, d)
    y = y + d_skip.astype(f32) * u32
    z = jax.nn.gelu(y).astype(u.dtype)
    val, gate = jnp.split(z @ w_glu, 2, axis=-1)
    return val * jax.nn.sigmoid(gate)


def diff_attention(h, pos, w_qkv, w_o, lq1, lk1, lq2, lk2, subln_g, lambda_init):
    bsz, seq_len, d = h.shape
    f32 = jnp.float32
    q, k, v = jnp.split(h @ w_qkv, 3, axis=-1)
    q = rope(q.reshape(bsz, seq_len, 2 * DA_HEADS, DA_HEAD_DIM), pos)
    k = rope(k.reshape(bsz, seq_len, 2 * DA_HEADS, DA_HEAD_DIM), pos)
    q = (q * (DA_HEAD_DIM ** -0.5)).reshape(bsz, seq_len, DA_HEADS, 2, DA_HEAD_DIM)
    k = k.reshape(bsz, seq_len, DA_HEADS, 2, DA_HEAD_DIM)
    v = v.reshape(bsz, seq_len, DA_HEADS, DA_V_DIM)
    lam = (jnp.exp(jnp.sum(lq1.astype(f32) * lk1.astype(f32)))
           - jnp.exp(jnp.sum(lq2.astype(f32) * lk2.astype(f32))) + lambda_init)
    outs = []
    for qb in range(seq_len // Q_BLOCK):
        q0, kend = qb * Q_BLOCK, (qb + 1) * Q_BLOCK
        s = jnp.einsum("bqhcd,bkhcd->bhcqk", q[:, q0:kend], k[:, :kend]).astype(f32)
        q_idx = q0 + jnp.arange(Q_BLOCK)
        chunk_end = (q_idx // CHUNK + 1) * CHUNK
        mask = jnp.arange(kend)[None, :] < chunk_end[:, None]
        p = jax.nn.softmax(jnp.where(mask, s, -jnp.inf), axis=-1)
        attn = (p[:, :, 0] - lam * p[:, :, 1]).astype(v.dtype)
        outs.append(jnp.einsum("bhqk,bkhe->bqhe", attn, v[:, :kend]))
    o = jnp.concatenate(outs, axis=1)
    o = rms_norm(o, subln_g) * (1.0 - lambda_init)
    return o.reshape(bsz, seq_len, d) @ w_o


def conv_ffn(h, w_in, conv_w, conv_b, w_out):
    seq_len = h.shape[1]
    u = h @ w_in
    up = jnp.pad(u, ((0, 0), (CONV_WIDTH - 1, 0), (0, 0)))
    u = conv_b + sum(conv_w[j] * up[:, j:j + seq_len] for j in range(CONV_WIDTH))
    a, b = jnp.split(u, 2, axis=-1)
    return (jax.nn.gelu(a) * b) @ w_out


def setup_inputs(seed: int = 0) -> dict:
    key = jax.random.key(seed)
    ks = jax.random.split(key, 26)
    f32 = jnp.float32

    def nrm(k, shape, s):
        return jax.random.normal(k, shape, f32) * s

    G, P, Cg = S5_GROUPS, S5_STATE, S5_GROUP
    x = nrm(ks[0], (BATCH, SEQ, D_MODEL), 1.0)
    c = nrm(ks[1], (BATCH, D_MODEL), 1.0)
    offset = jax.random.randint(ks[2], (BATCH, 1), 0, MAX_POS_OFFSET, dtype=jnp.int32)
    positions = offset + jnp.arange(SEQ, dtype=jnp.int32)[None, :]
    ada_w = nrm(ks[3], (DEPTH, D_MODEL, 6 * D_MODEL), 0.5 * D_MODEL ** -0.5)
    ada_b = nrm(ks[4], (DEPTH, 6 * D_MODEL), 0.02)
    norm_g = 1.0 + nrm(ks[5], (DEPTH, 4, D_MODEL), 0.05)
    s5_a_re = -0.5 + nrm(ks[6], (N_S5, G, P), 0.01)
    s5_a_im = math.pi * jnp.arange(P, dtype=f32) + nrm(ks[7], (N_S5, G, P), 0.01)
    s5_log_dt = jax.random.uniform(ks[8], (N_S5, G), f32, math.log(S5_DT_MIN), math.log(S5_DT_MAX))
    s5_b_re = nrm(ks[9], (N_S5, G, P, Cg), (2 * Cg) ** -0.5)
    s5_b_im = nrm(ks[10], (N_S5, G, P, Cg), (2 * Cg) ** -0.5)
    s5_c_re = nrm(ks[11], (N_S5, G, Cg, P), P ** -0.5)
    s5_c_im = nrm(ks[12], (N_S5, G, Cg, P), P ** -0.5)
    s5_d = nrm(ks[13], (N_S5, D_MODEL), 1.0)
    s5_w_glu = nrm(ks[14], (N_S5, D_MODEL, 2 * D_MODEL), D_MODEL ** -0.5)
    da_w_qkv = nrm(ks[15], (N_DA, D_MODEL, 3 * D_MODEL), D_MODEL ** -0.5)
    da_w_o = nrm(ks[16], (N_DA, D_MODEL, D_MODEL), D_MODEL ** -0.5)
    da_lq1 = nrm(ks[17], (N_DA, DA_HEAD_DIM), 0.1)
    da_lk1 = nrm(ks[18], (N_DA, DA_HEAD_DIM), 0.1)
    da_lq2 = nrm(ks[19], (N_DA, DA_HEAD_DIM), 0.1)
    da_lk2 = nrm(ks[20], (N_DA, DA_HEAD_DIM), 0.1)
    da_subln_g = 1.0 + nrm(ks[21], (N_DA, DA_V_DIM), 0.05)
    ffn_w_in = nrm(ks[22], (DEPTH, D_MODEL, 2 * D_FF), D_MODEL ** -0.5)
    ffn_conv_w = nrm(ks[23], (DEPTH, CONV_WIDTH, 2 * D_FF), CONV_WIDTH ** -0.5)
    ffn_conv_b = nrm(ks[24], (DEPTH, 2 * D_FF), 0.02)
    ffn_w_out = nrm(ks[25], (DEPTH, D_FF, D_MODEL), D_FF ** -0.5)
    return {"x": x, "c": c, "positions": positions, "ada_w": ada_w, "ada_b": ada_b,
            "norm_g": norm_g, "s5_a_re": s5_a_re, "s5_a_im": s5_a_im, "s5_log_dt": s5_log_dt,
            "s5_b_re": s5_b_re, "s5_b_im": s5_b_im, "s5_c_re": s5_c_re, "s5_c_im": s5_c_im,
            "s5_d": s5_d, "s5_w_glu": s5_w_glu, "da_w_qkv": da_w_qkv, "da_w_o": da_w_o,
            "da_lq1": da_lq1, "da_lk1": da_lk1, "da_lq2": da_lq2, "da_lk2": da_lk2,
            "da_subln_g": da_subln_g, "ffn_w_in": ffn_w_in, "ffn_conv_w": ffn_conv_w,
            "ffn_conv_b": ffn_conv_b, "ffn_w_out": ffn_w_out}


def reference(x, c, positions, ada_w, ada_b, norm_g, s5_a_re, s5_a_im, s5_log_dt,
              s5_b_re, s5_b_im, s5_c_re, s5_c_im, s5_d, s5_w_glu, da_w_qkv, da_w_o,
              da_lq1, da_lk1, da_lq2, da_lk2, da_subln_g, ffn_w_in, ffn_conv_w,
              ffn_conv_b, ffn_w_out):
    cond = jax.nn.silu(c)
    for i in range(DEPTH):
        mod = cond @ ada_w[i] + ada_b[i]
        sh_t, sc_t, g_t, sh_c, sc_c, g_c = jnp.split(mod, 6, axis=-1)
        h = modulate(rms_norm(x, norm_g[i, 0]), sh_t, sc_t)
        j = i // N_MIXERS
        if i % N_MIXERS == 0:
            h = s5_mixer(h, s5_a_re[j], s5_a_im[j], s5_log_dt[j], s5_b_re[j], s5_b_im[j],
                         s5_c_re[j], s5_c_im[j], s5_d[j], s5_w_glu[j])
        else:
            lambda_init = 0.8 - 0.6 * math.exp(-0.3 * i)
            h = diff_attention(h, positions, da_w_qkv[j], da_w_o[j], da_lq1[j], da_lk1[j],
                               da_lq2[j], da_lk2[j], da_subln_g[j], lambda_init)
        x = x + g_t[:, None, :] * rms_norm(h, norm_g[i, 1])
        h = modulate(rms_norm(x, norm_g[i, 2]), sh_c, sc_c)
        h = conv_ffn(h, ffn_w_in[i], ffn_conv_w[i], ffn_conv_b[i], ffn_w_out[i])
        x = x + g_c[:, None, :] * rms_norm(h, norm_g[i, 3])
    return x
```

```python
import functools
import math

import jax
import jax.numpy as jnp
from jax import lax
from jax.experimental import pallas as pl
from jax.experimental.pallas import tpu as pltpu

F32 = jnp.float32
BF16 = jnp.bfloat16

EPS = 1e-6
CHUNK = 64
S5_GROUP = 16
S5_STATE = 64
S5_LAMBDA_RE_MAX = -1e-4
DA_HEADS = 8
DA_HEAD_DIM = 64
DA_V_DIM = 2 * DA_HEAD_DIM
ROPE_THETA = 10000.0
CONV_WIDTH = 3

LANES = 128
MXU_DIM = 256
VMEM_LIMIT_BYTES = 56 * 1024 * 1024

S5_PACK_GROUPS = MXU_DIM // S5_GROUP
S5_PACK_STATES = S5_PACK_GROUPS * S5_STATE
S5_SCAN_LANES = 4 * LANES

ROW_TILE = 512
ATTN_TQ = 256
FFN_TF = 256
NEG_BIG = -1e30


def _params(*sem):
    return pltpu.CompilerParams(dimension_semantics=sem, vmem_limit_bytes=VMEM_LIMIT_BYTES)


def _const_spec(shape):
    nd = len(shape)
    return pl.BlockSpec(shape, lambda *_: (0,) * nd, pipeline_mode=pl.Buffered(1))


def _row_spec(tm, width):
    return pl.BlockSpec((tm, width), lambda i: (i, 0))


def _rms(x, g):
    ms = jnp.mean(x * x, axis=-1, keepdims=True)
    return x * lax.rsqrt(ms + EPS) * g


def _per_seq(x, bsz, fn):
    rows, d = x.shape
    return fn(x.reshape(rows // bsz, bsz, d)).reshape(rows, d)


def _modulated_norm(x, g, sh, sc):
    return _per_seq(_rms(x, g), sh.shape[0], lambda y: y * (1.0 + sc) + sh)


def _gated_residual(x, y, g, gate):
    return x + _per_seq(_rms(y, g), gate.shape[0], lambda r: gate * r)


def _dot(a, b):
    return jnp.dot(a, b, preferred_element_type=F32)


def _ada_kernel(c_ref, w_ref, b_ref, o_ref):
    c = c_ref[...]
    cond = (c * jax.nn.sigmoid(c)).astype(BF16)
    o_ref[...] = _dot(cond, w_ref[...].astype(BF16)) + b_ref[...]


def _ada_mod(c, ada_w, ada_b):
    depth, d, n = ada_w.shape
    bsz = c.shape[0]
    tn = n // 4
    return pl.pallas_call(
        _ada_kernel,
        grid=(depth, n // tn),
        in_specs=[
            pl.BlockSpec((bsz, d), lambda i, j: (0, 0)),
            pl.BlockSpec((None, d, tn), lambda i, j: (i, 0, j)),
            pl.BlockSpec((None, 1, tn), lambda i, j: (i, 0, j)),
        ],
        out_specs=pl.BlockSpec((None, bsz, tn), lambda i, j: (i, 0, j)),
        out_shape=jax.ShapeDtypeStruct((depth, bsz, n), F32),
        compiler_params=_params("parallel", "parallel"),
        name="ada_mod",
    )(c, ada_w, ada_b.reshape(depth, 1, n))


def _s5_disc_kernel(are_ref, aim_ref, ldt_ref, br_ref, bi_ref, lbr_ref, lbi_ref, bbr_ref, bbi_ref):
    lam_re = jnp.minimum(are_ref[...], S5_LAMBDA_RE_MAX)
    lam_im = aim_ref[...]
    dt = jnp.exp(ldt_ref[...])
    dre, dimg = lam_re * dt, lam_im * dt
    mag = jnp.exp(dre)
    lb_re, lb_im = mag * jnp.cos(dimg), mag * jnp.sin(dimg)
    den = lam_re * lam_re + lam_im * lam_im
    nr = lb_re - 1.0
    f_re = (nr * lam_re + lb_im * lam_im) / den
    f_im = (lb_im * lam_re - nr * lam_im) / den
    br, bi = br_ref[...], bi_ref[...]
    lbr_ref[...] = lb_re
    lbi_ref[...] = lb_im
    bbr_ref[...] = f_re * br - f_im * bi
    bbi_ref[...] = f_re * bi + f_im * br


def _s5_discretise(a_re, a_im, log_dt, b_re, b_im):
    n, g, p = a_re.shape
    cg = b_re.shape[-1]
    vec = pl.BlockSpec((None, g, 1, p), lambda i: (i, 0, 0, 0))
    mat = pl.BlockSpec((None, g, cg, p), lambda i: (i, 0, 0, 0))
    return pl.pallas_call(
        _s5_disc_kernel,
        grid=(n,),
        in_specs=[vec, vec, pl.BlockSpec((None, g, 1, 1), lambda i: (i, 0, 0, 0)), mat, mat],
        out_specs=[vec, vec, mat, mat],
        out_shape=[jax.ShapeDtypeStruct((n, g, 1, p), F32)] * 2
        + [jax.ShapeDtypeStruct((n, g, cg, p), F32)] * 2,
        compiler_params=_params("parallel"),
        name="s5_discretise",
    )(a_re.reshape(n, g, 1, p), a_im.reshape(n, g, 1, p), log_dt.reshape(n, g, 1, 1),
      jnp.swapaxes(b_re, -1, -2), jnp.swapaxes(b_im, -1, -2))


def _s5_pack_weights(lb_re, lb_im, bb_re, bb_im, c_re, c_im):
    g, cg, p = bb_re.shape
    gp = S5_PACK_GROUPS
    npack = g // gp
    eye = jnp.eye(gp, dtype=F32)

    def in_proj(bb):
        return jnp.einsum("kgcp,gh->kgchp", bb.reshape(npack, gp, cg, p), eye).reshape(
            npack, gp * cg, gp * p)

    def out_proj(c):
        return jnp.einsum("kgcp,gh->kgphc", c.reshape(npack, gp, cg, p), eye).reshape(
            npack, gp * p, gp * cg)

    wb = jnp.concatenate([in_proj(bb_re), in_proj(bb_im)], axis=-1).astype(BF16)
    return (wb, out_proj(c_re).astype(BF16), out_proj(c_im).astype(BF16),
            lb_re.reshape(npack, 1, gp * p), lb_im.reshape(npack, 1, gp * p))


def _s5_kernel(x_ref, g0_ref, sh_ref, sc_ref, gate_ref, g1_ref, wb_ref, lbr_ref, lbi_ref,
               wcr_ref, wci_ref, dsk_ref, wglu_ref, out_ref, h_ref, u_ref, bu_ref, y_ref):
    rows, d = x_ref.shape
    bsz = sh_ref.shape[0]
    npack = wb_ref.shape[0]
    ns = S5_PACK_STATES

    @pl.when(pl.program_id(0) == 0)
    def _():
        h_ref[...] = jnp.zeros_like(h_ref)

    u_ref[...] = _modulated_norm(x_ref[...], g0_ref[...], sh_ref[...], sc_ref[...])

    for k in range(npack):
        c0 = k * MXU_DIM
        bu_ref[...] = _dot(u_ref[:, c0:c0 + MXU_DIM].astype(BF16), wb_ref[k])
        for s in range(ns // S5_SCAN_LANES):
            re = pl.ds(s * S5_SCAN_LANES, S5_SCAN_LANES)
            im = pl.ds(ns + s * S5_SCAN_LANES, S5_SCAN_LANES)
            lbr = jnp.broadcast_to(lbr_ref[k, :, re], (bsz, S5_SCAN_LANES))
            lbi = jnp.broadcast_to(lbi_ref[k, :, re], (bsz, S5_SCAN_LANES))

            def body(t, carry, re=re, im=im, lbr=lbr, lbi=lbi):
                hr, hi = carry
                frame = pl.ds(pl.multiple_of(t * bsz, bsz), bsz)
                nr = lbr * hr - lbi * hi + bu_ref[frame, re]
                ni = lbr * hi + lbi * hr + bu_ref[frame, im]
                bu_ref[frame, re] = nr
                bu_ref[frame, im] = ni
                return nr, ni

            hr, hi = lax.fori_loop(0, rows // bsz, body, (h_ref[k, :, re], h_ref[k, :, im]),
                                   unroll=8)
            h_ref[k, :, re] = hr
            h_ref[k, :, im] = hi
        y_ref[:, c0:c0 + MXU_DIM] = (_dot(bu_ref[:, :ns].astype(BF16), wcr_ref[k])
                                     - _dot(bu_ref[:, ns:].astype(BF16), wci_ref[k]))

    y = y_ref[...] + dsk_ref[...] * u_ref[...]
    o = _dot(jax.nn.gelu(y).astype(BF16), wglu_ref[...])
    mix = o[:, :d] * jax.nn.sigmoid(o[:, d:])
    out_ref[...] = _gated_residual(x_ref[...], mix, g1_ref[...], gate_ref[...])


def _s5_layer(x, g0, sh, sc, gate, g1, packed, d_skip, w_glu):
    n, d = x.shape
    bsz = sh.shape[0]
    wb, wcr, wci, lbr, lbi = packed
    npack = wb.shape[0]
    rows = CHUNK * bsz
    return pl.pallas_call(
        _s5_kernel,
        grid=(n // rows,),
        in_specs=[_row_spec(rows, d), _const_spec((1, d)), _const_spec((bsz, d)),
                  _const_spec((bsz, d)), _const_spec((bsz, d)), _const_spec((1, d)),
                  _const_spec(wb.shape), _const_spec(lbr.shape), _const_spec(lbi.shape),
                  _const_spec(wcr.shape), _const_spec(wci.shape), _const_spec((1, d)),
                  _const_spec(w_glu.shape)],
        out_specs=_row_spec(rows, d),
        out_shape=jax.ShapeDtypeStruct(x.shape, F32),
        scratch_shapes=[pltpu.VMEM((npack, bsz, 2 * S5_PACK_STATES), F32),
                        pltpu.VMEM((rows, d), F32),
                        pltpu.VMEM((rows, 2 * S5_PACK_STATES), F32),
                        pltpu.VMEM((rows, d), F32)],
        compiler_params=_params("arbitrary"),
        name="s5_layer",
    )(x, g0, sh, sc, gate, g1, wb, lbr, lbi, wcr, wci, d_skip.reshape(1, d), w_glu)


def _rope_kernel(pos_ref, inv_ref, cos_ref, sin_ref):
    ang = pos_ref[...].astype(F32) * inv_ref[...]
    lane = lax.broadcasted_iota(jnp.int32, ang.shape, 1)
    first_half = lane % DA_HEAD_DIM < DA_HEAD_DIM // 2
    cos_ref[...] = jnp.cos(ang)
    sin_ref[...] = jnp.where(first_half, -1.0, 1.0) * jnp.sin(ang)


def _rope_tables(pos_rows):
    n = pos_rows.shape[0]
    half = DA_HEAD_DIM // 2
    inv = ROPE_THETA ** (-jnp.arange(half, dtype=F32) / half)
    inv = jnp.tile(inv, LANES // half).reshape(1, LANES)
    tm = min(n, ROW_TILE)
    return pl.pallas_call(
        _rope_kernel,
        grid=(n // tm,),
        in_specs=[_row_spec(tm, 1), pl.BlockSpec((1, LANES), lambda i: (0, 0))],
        out_specs=[_row_spec(tm, LANES)] * 2,
        out_shape=[jax.ShapeDtypeStruct((n, LANES), F32)] * 2,
        compiler_params=_params("parallel"),
        name="rope_tables",
    )(pos_rows, inv)


def _qkv_kernel(x_ref, g_ref, sh_ref, sc_ref, w_ref, cos_ref, sin_ref, q_ref, k_ref, v_ref):
    d = x_ref.shape[-1]
    h = _modulated_norm(x_ref[...], g_ref[...], sh_ref[...], sc_ref[...]).astype(BF16)
    cos, sin = cos_ref[...], sin_ref[...]
    lane = lax.broadcasted_iota(jnp.int32, cos.shape, 1)
    half = DA_HEAD_DIM // 2
    first_half = lane % DA_HEAD_DIM < half
    for src, dst, scale in ((0, q_ref, DA_HEAD_DIM ** -0.5), (d, k_ref, 1.0)):
        t = _dot(h, w_ref[:, src:src + d])
        for j in range(d // LANES):
            tj = t[:, j * LANES:(j + 1) * LANES]
            rot = jnp.where(first_half, pltpu.roll(tj, LANES - half, 1), pltpu.roll(tj, half, 1))
            dst[:, j * LANES:(j + 1) * LANES] = ((tj * cos + rot * sin) * scale).astype(BF16)
    v_ref[...] = _dot(h, w_ref[:, 2 * d:]).astype(BF16)


def _qkv_proj(x, g, sh, sc, w_qkv, cos, sin):
    n, d = x.shape
    bsz = sh.shape[0]
    tm = min(n, ROW_TILE)
    return pl.pallas_call(
        _qkv_kernel,
        grid=(n // tm,),
        in_specs=[_row_spec(tm, d), _const_spec((1, d)), _const_spec((bsz, d)),
                  _const_spec((bsz, d)), _const_spec(w_qkv.shape), _row_spec(tm, LANES),
                  _row_spec(tm, LANES)],
        out_specs=[_row_spec(tm, d)] * 3,
        out_shape=[jax.ShapeDtypeStruct((n, d), BF16)] * 3,
        compiler_params=_params("parallel"),
        name="qkv_proj",
    )(x, g, sh, sc, w_qkv, cos, sin)


def _attn_kernel(q_ref, k_ref, v_ref, lq1_ref, lk1_ref, lq2_ref, lk2_ref, sg_ref, o_ref,
                 vext_ref, acc_ref, *, lambda_init, tq):
    seq = q_ref.shape[0]
    vext_ref[:, :DA_V_DIM] = v_ref[...]
    col = lax.broadcasted_iota(jnp.int32, (seq, DA_V_DIM), 1)
    vext_ref[:, DA_V_DIM:] = jnp.where(col == 0, 1.0, 0.0).astype(BF16)

    lam = (jnp.exp(jnp.sum(lq1_ref[...] * lk1_ref[...], keepdims=True))
           - jnp.exp(jnp.sum(lq2_ref[...] * lk2_ref[...], keepdims=True)) + lambda_init)

    lane = lax.broadcasted_iota(jnp.int32, (tq, 2 * DA_HEAD_DIM), 1)
    first = lane < DA_HEAD_DIM
    r = lax.broadcasted_iota(jnp.int32, (2 * tq, tq), 0)
    c = lax.broadcasted_iota(jnp.int32, (2 * tq, tq), 1)
    qpos = jnp.where(r >= tq, r - tq, r)
    visible = c < (qpos // CHUNK + 1) * CHUNK

    def q_block(qi, carry):
        q0 = pl.multiple_of(qi * tq, tq)
        qb = q_ref[pl.ds(q0, tq), :]
        zero = jnp.zeros_like(qb)
        qs = jnp.concatenate([jnp.where(first, qb, zero), jnp.where(first, zero, qb)], axis=0)
        acc_ref[...] = jnp.zeros_like(acc_ref)

        def step(k0, m, masked):
            s = lax.dot_general(qs, k_ref[pl.ds(k0, tq), :], (((1,), (1,)), ((), ())),
                                preferred_element_type=F32)
            if masked:
                s = jnp.where(visible, s, NEG_BIG)
            m_new = jnp.maximum(m, jnp.max(s, axis=-1, keepdims=True))
            p = jnp.exp(s - m_new).astype(BF16)
            acc_ref[...] = jnp.exp(m - m_new) * acc_ref[...] + _dot(p, vext_ref[pl.ds(k0, tq), :])
            return m_new

        m = lax.fori_loop(0, qi, lambda kj, m: step(pl.multiple_of(kj * tq, tq), m, False),
                          jnp.full((2 * tq, 1), NEG_BIG, F32))
        step(q0, m, True)
        acc = acc_ref[...]
        o = acc[:, :DA_V_DIM] / acc[:, DA_V_DIM:DA_V_DIM + 1]
        od = o[:tq] - lam * o[tq:]
        od = _rms(od, sg_ref[...]) * (1.0 - lambda_init)
        o_ref[pl.ds(q0, tq), :] = od.astype(BF16)
        return carry

    lax.fori_loop(0, seq // tq, q_block, 0)


def _diff_attention(q, k, v, lq1, lk1, lq2, lk2, subln_g, lambda_init):
    seq, width = q.shape
    tq = min(seq, ATTN_TQ)
    head = pl.BlockSpec((seq, DA_V_DIM), lambda j: (0, j))
    vec = _const_spec((1, DA_HEAD_DIM))
    return pl.pallas_call(
        functools.partial(_attn_kernel, lambda_init=lambda_init, tq=tq),
        grid=(width // DA_V_DIM,),
        in_specs=[head, head, head, vec, vec, vec, vec, _const_spec((1, DA_V_DIM))],
        out_specs=head,
        out_shape=jax.ShapeDtypeStruct((seq, width), BF16),
        scratch_shapes=[pltpu.VMEM((seq, 2 * DA_V_DIM), BF16),
                        pltpu.VMEM((2 * tq, 2 * DA_V_DIM), F32)],
        compiler_params=_params("parallel"),
        name="diff_attention",
    )(q, k, v, lq1.reshape(1, -1), lk1.reshape(1, -1), lq2.reshape(1, -1), lk2.reshape(1, -1),
      subln_g.reshape(1, -1))


def _oproj_kernel(o_ref, x_ref, w_ref, g_ref, gate_ref, out_ref):
    y = _dot(o_ref[...], w_ref[...])
    out_ref[...] = _gated_residual(x_ref[...], y, g_ref[...], gate_ref[...])


def _out_proj(o, x, w_o, g, gate):
    n, d = x.shape
    bsz = gate.shape[0]
    tm = min(n, ROW_TILE)
    return pl.pallas_call(
        _oproj_kernel,
        grid=(n // tm,),
        in_specs=[_row_spec(tm, d), _row_spec(tm, d), _const_spec(w_o.shape), _const_spec((1, d)),
                  _const_spec((bsz, d))],
        out_specs=_row_spec(tm, d),
        out_shape=jax.ShapeDtypeStruct(x.shape, F32),
        compiler_params=_params("parallel"),
        name="attn_out_proj",
    )(o, x, w_o, g, gate)


def _ffn_kernel(x_ref, xprev_ref, g2_ref, sh_ref, sc_ref, gate_ref, g3_ref, win_ref, cw_ref, cb_ref,
                wout_ref, out_ref, h_ref, acc_ref):
    tm, d = x_ref.shape
    f = wout_ref.shape[0]
    bsz = sh_ref.shape[0]
    halo = xprev_ref.shape[0]
    g2, sh, sc = g2_ref[...], sh_ref[...], sc_ref[...]
    h_ref[:halo, :] = _modulated_norm(xprev_ref[...], g2, sh, sc).astype(BF16)
    h_ref[halo:, :] = _modulated_norm(x_ref[...], g2, sh, sc).astype(BF16)
    row = lax.broadcasted_iota(jnp.int32, (tm + halo, FFN_TF), 0)
    keep = jnp.logical_or(row >= halo, pl.program_id(0) > 0)
    acc_ref[...] = jnp.zeros_like(acc_ref)

    def conv(u, col):
        u = jnp.where(keep, u, 0.0)
        w = cw_ref[:, col]
        out = cb_ref[:, col]
        for j in range(CONV_WIDTH):
            out = out + w[j:j + 1] * u[j * bsz:j * bsz + tm]
        return out

    def body(i, carry):
        c0 = pl.multiple_of(i * FFN_TF, FFN_TF)
        ca, cb = pl.ds(c0, FFN_TF), pl.ds(f + c0, FFN_TF)
        hb = h_ref[...]
        a = conv(_dot(hb, win_ref[:, ca]), ca)
        b = conv(_dot(hb, win_ref[:, cb]), cb)
        acc_ref[...] += _dot((jax.nn.gelu(a) * b).astype(BF16), wout_ref[ca, :])
        return carry

    lax.fori_loop(0, f // FFN_TF, body, 0)
    out_ref[...] = _gated_residual(x_ref[...], acc_ref[...], g3_ref[...], gate_ref[...])


def _conv_ffn(x, g2, sh, sc, gate, g3, w_in, conv_w, conv_b, w_out):
    n, d = x.shape
    f = w_out.shape[0]
    bsz = sh.shape[0]
    tm = min(n, ROW_TILE)
    halo = (CONV_WIDTH - 1) * bsz
    prev = pl.BlockSpec((halo, d), lambda i: (jnp.maximum(i * (tm // halo) - 1, 0), 0))
    vec = _const_spec((bsz, d))
    return pl.pallas_call(
        _ffn_kernel,
        grid=(n // tm,),
        in_specs=[_row_spec(tm, d), prev, _const_spec((1, d)), vec, vec, vec, _const_spec((1, d)),
                  _const_spec(w_in.shape), _const_spec(conv_w.shape), _const_spec((1, 2 * f)),
                  _const_spec(w_out.shape)],
        out_specs=_row_spec(tm, d),
        out_shape=jax.ShapeDtypeStruct(x.shape, F32),
        scratch_shapes=[pltpu.VMEM((tm + halo, d), BF16), pltpu.VMEM((tm, d), F32)],
        compiler_params=_params("parallel"),
        name="conv_ffn",
    )(x, x, g2, sh, sc, gate, g3, w_in, conv_w, conv_b.reshape(1, 2 * f), w_out)


def kernel(x, c, positions, ada_w, ada_b, norm_g, s5_a_re, s5_a_im, s5_log_dt, s5_b_re, s5_b_im,
           s5_c_re, s5_c_im, s5_d, s5_w_glu, da_w_qkv, da_w_o, da_lq1, da_lk1, da_lq2, da_lk2,
           da_subln_g, ffn_w_in, ffn_conv_w, ffn_conv_b, ffn_w_out):
    depth = ada_w.shape[0]
    bsz, seq, d = x.shape
    n = seq * bsz
    mod = _ada_mod(c, ada_w, ada_b).reshape(depth, bsz, 6, d)
    lb_re, lb_im, bb_re, bb_im = _s5_discretise(s5_a_re, s5_a_im, s5_log_dt, s5_b_re, s5_b_im)
    cos, sin = _rope_tables(positions.T.reshape(n, 1))
    x = jnp.swapaxes(x, 0, 1).reshape(n, d)
    for i in range(depth):
        sh_t, sc_t, g_t, sh_c, sc_c, g_c = (mod[i, :, m] for m in range(6))
        gains = norm_g[i].reshape(4, 1, d)
        j = i // 2
        if i % 2 == 0:
            packed = _s5_pack_weights(lb_re[j], lb_im[j], bb_re[j], bb_im[j], s5_c_re[j], s5_c_im[j])
            x = _s5_layer(x, gains[0], sh_t, sc_t, g_t, gains[1], packed, s5_d[j],
                          s5_w_glu[j].astype(BF16))
        else:
            lambda_init = 0.8 - 0.6 * math.exp(-0.3 * i)
            q, k, v = _qkv_proj(x, gains[0], sh_t, sc_t, da_w_qkv[j].astype(BF16), cos, sin)
            o = _diff_attention(q.reshape(seq, bsz * d), k.reshape(seq, bsz * d),
                                v.reshape(seq, bsz * d), da_lq1[j], da_lk1[j], da_lq2[j],
                                da_lk2[j], da_subln_g[j], lambda_init)
            x = _out_proj(o.reshape(n, d), x, da_w_o[j].astype(BF16), gains[1], g_t)
        x = _conv_ffn(x, gains[2], sh_c, sc_c, g_c, gains[3], ffn_w_in[i].astype(BF16),
                      ffn_conv_w[i], ffn_conv_b[i], ffn_w_out[i].astype(BF16))
    return jnp.swapaxes(x.reshape(seq, bsz, d), 0, 1)
```

```python
import functools
import math

import jax
import jax.numpy as jnp
from jax import lax
from jax.experimental import pallas as pl
from jax.experimental.pallas import tpu as pltpu

F32 = jnp.float32
BF16 = jnp.bfloat16

EPS = 1e-6
CHUNK = 64
S5_GROUP = 16
S5_STATE = 64
S5_LAMBDA_RE_MAX = -1e-4
DA_HEADS = 8
DA_HEAD_DIM = 64
DA_V_DIM = 2 * DA_HEAD_DIM
ROPE_THETA = 10000.0
CONV_WIDTH = 3

LANES = 128
MXU_DIM = 256
VMEM_LIMIT_BYTES = 56 * 1024 * 1024

S5_PACK_GROUPS = MXU_DIM // S5_GROUP
S5_PACK_STATES = S5_PACK_GROUPS * S5_STATE
S5_SCAN_LANES = 4 * LANES

ROW_TILE = 512
ATTN_TQ = 512
ATTN_MM_ROWS = 256
ATTN_SM_ROWS = 128
FFN_TF = 256
NEG_BIG = -1e30


def _params(*sem):
    return pltpu.CompilerParams(dimension_semantics=sem, vmem_limit_bytes=VMEM_LIMIT_BYTES)


def _const_spec(shape):
    nd = len(shape)
    return pl.BlockSpec(shape, lambda *_: (0,) * nd, pipeline_mode=pl.Buffered(1))


def _row_spec(tm, width):
    return pl.BlockSpec((tm, width), lambda i: (i, 0))


def _rms(x, g):
    ms = jnp.mean(x * x, axis=-1, keepdims=True)
    return x * lax.rsqrt(ms + EPS) * g


def _per_seq(x, bsz, fn):
    rows, d = x.shape
    if bsz == 1:
        return fn(x)
    return fn(x.reshape(rows // bsz, bsz, d)).reshape(rows, d)


def _modulated_norm(x, g, sh, sc):
    return _per_seq(_rms(x, g), sh.shape[0], lambda y: y * (1.0 + sc) + sh)


def _gated_residual(x, y, g, gate):
    return x + _per_seq(_rms(y, g), gate.shape[0], lambda r: gate * r)


def _dot(a, b):
    return jnp.dot(a, b, preferred_element_type=F32)


def _ada_kernel(c_ref, w_ref, b_ref, o_ref):
    c = c_ref[...]
    cond = (c * jax.nn.sigmoid(c)).astype(BF16)
    o_ref[...] = _dot(cond, w_ref[...].astype(BF16)) + b_ref[...]


def _ada_mod(c, ada_w, ada_b):
    depth, d, n = ada_w.shape
    bsz = c.shape[0]
    tn = n // 4
    return pl.pallas_call(
        _ada_kernel,
        grid=(depth, n // tn),
        in_specs=[
            pl.BlockSpec((bsz, d), lambda i, j: (0, 0)),
            pl.BlockSpec((None, d, tn), lambda i, j: (i, 0, j)),
            pl.BlockSpec((None, 1, tn), lambda i, j: (i, 0, j)),
        ],
        out_specs=pl.BlockSpec((None, bsz, tn), lambda i, j: (i, 0, j)),
        out_shape=jax.ShapeDtypeStruct((depth, bsz, n), F32),
        compiler_params=_params("parallel", "parallel"),
        name="ada_mod",
    )(c, ada_w, ada_b.reshape(depth, 1, n))


def _s5_disc_kernel(are_ref, aim_ref, ldt_ref, br_ref, bi_ref, lbr_ref, lbi_ref, bbr_ref, bbi_ref):
    lam_re = jnp.minimum(are_ref[...], S5_LAMBDA_RE_MAX)
    lam_im = aim_ref[...]
    dt = jnp.exp(ldt_ref[...])
    dre, dimg = lam_re * dt, lam_im * dt
    mag = jnp.exp(dre)
    lb_re, lb_im = mag * jnp.cos(dimg), mag * jnp.sin(dimg)
    den = lam_re * lam_re + lam_im * lam_im
    nr = lb_re - 1.0
    f_re = (nr * lam_re + lb_im * lam_im) / den
    f_im = (lb_im * lam_re - nr * lam_im) / den
    br, bi = br_ref[...], bi_ref[...]
    lbr_ref[...] = lb_re
    lbi_ref[...] = lb_im
    bbr_ref[...] = f_re * br - f_im * bi
    bbi_ref[...] = f_re * bi + f_im * br


def _s5_discretise(a_re, a_im, log_dt, b_re, b_im):
    n, g, p = a_re.shape
    cg = b_re.shape[-1]
    vec = pl.BlockSpec((None, g, 1, p), lambda i: (i, 0, 0, 0))
    mat = pl.BlockSpec((None, g, cg, p), lambda i: (i, 0, 0, 0))
    return pl.pallas_call(
        _s5_disc_kernel,
        grid=(n,),
        in_specs=[vec, vec, pl.BlockSpec((None, g, 1, 1), lambda i: (i, 0, 0, 0)), mat, mat],
        out_specs=[vec, vec, mat, mat],
        out_shape=[jax.ShapeDtypeStruct((n, g, 1, p), F32)] * 2
        + [jax.ShapeDtypeStruct((n, g, cg, p), F32)] * 2,
        compiler_params=_params("parallel"),
        name="s5_discretise",
    )(a_re.reshape(n, g, 1, p), a_im.reshape(n, g, 1, p), log_dt.reshape(n, g, 1, 1),
      jnp.swapaxes(b_re, -1, -2), jnp.swapaxes(b_im, -1, -2))


def _s5_pack_weights(lb_re, lb_im, bb_re, bb_im, c_re, c_im):
    g, cg, p = bb_re.shape
    gp = S5_PACK_GROUPS
    npack = g // gp
    eye = jnp.eye(gp, dtype=F32)

    def in_proj(bb):
        return jnp.einsum("kgcp,gh->kgchp", bb.reshape(npack, gp, cg, p), eye).reshape(
            npack, gp * cg, gp * p)

    def out_proj(c):
        return jnp.einsum("kgcp,gh->kgphc", c.reshape(npack, gp, cg, p), eye).reshape(
            npack, gp * p, gp * cg)

    wb = jnp.concatenate([in_proj(bb_re), in_proj(bb_im)], axis=-1).astype(BF16)
    return (wb, out_proj(c_re).astype(BF16), out_proj(c_im).astype(BF16),
            lb_re.reshape(npack, 1, gp * p), lb_im.reshape(npack, 1, gp * p))


def _s5_kernel(x_ref, g0_ref, sh_ref, sc_ref, gate_ref, g1_ref, wb_ref, lbr_ref, lbi_ref,
               wcr_ref, wci_ref, dsk_ref, wglu_ref, out_ref, h_ref, u_ref, bu_ref, y_ref):
    rows, d = x_ref.shape
    bsz = sh_ref.shape[0]
    npack = wb_ref.shape[0]
    ns = S5_PACK_STATES

    @pl.when(pl.program_id(0) == 0)
    def _():
        h_ref[...] = jnp.zeros_like(h_ref)

    u_ref[...] = _modulated_norm(x_ref[...], g0_ref[...], sh_ref[...], sc_ref[...])

    for k in range(npack):
        c0 = k * MXU_DIM
        bu_ref[...] = _dot(u_ref[:, c0:c0 + MXU_DIM].astype(BF16), wb_ref[k])
        for s in range(ns // S5_SCAN_LANES):
            re = pl.ds(s * S5_SCAN_LANES, S5_SCAN_LANES)
            im = pl.ds(ns + s * S5_SCAN_LANES, S5_SCAN_LANES)
            lbr = jnp.broadcast_to(lbr_ref[k, :, re], (bsz, S5_SCAN_LANES))
            lbi = jnp.broadcast_to(lbi_ref[k, :, re], (bsz, S5_SCAN_LANES))

            def body(t, carry, re=re, im=im, lbr=lbr, lbi=lbi):
                hr, hi = carry
                frame = pl.ds(pl.multiple_of(t * bsz, bsz), bsz)
                nr = lbr * hr - lbi * hi + bu_ref[frame, re]
                ni = lbr * hi + lbi * hr + bu_ref[frame, im]
                bu_ref[frame, re] = nr
                bu_ref[frame, im] = ni
                return nr, ni

            hr, hi = lax.fori_loop(0, rows // bsz, body, (h_ref[k, :, re], h_ref[k, :, im]),
                                   unroll=8)
            h_ref[k, :, re] = hr
            h_ref[k, :, im] = hi
        y_ref[:, c0:c0 + MXU_DIM] = (_dot(bu_ref[:, :ns].astype(BF16), wcr_ref[k])
                                     - _dot(bu_ref[:, ns:].astype(BF16), wci_ref[k]))

    y = y_ref[...] + dsk_ref[...] * u_ref[...]
    o = _dot(jax.nn.gelu(y).astype(BF16), wglu_ref[...])
    mix = o[:, :d] * jax.nn.sigmoid(o[:, d:])
    out_ref[...] = _gated_residual(x_ref[...], mix, g1_ref[...], gate_ref[...])


def _s5_layer(x, g0, sh, sc, gate, g1, packed, d_skip, w_glu):
    n, d = x.shape
    bsz = sh.shape[0]
    wb, wcr, wci, lbr, lbi = packed
    npack = wb.shape[0]
    rows = CHUNK * bsz
    return pl.pallas_call(
        _s5_kernel,
        grid=(n // rows,),
        in_specs=[_row_spec(rows, d), _const_spec((1, d)), _const_spec((bsz, d)),
                  _const_spec((bsz, d)), _const_spec((bsz, d)), _const_spec((1, d)),
                  _const_spec(wb.shape), _const_spec(lbr.shape), _const_spec(lbi.shape),
                  _const_spec(wcr.shape), _const_spec(wci.shape), _const_spec((1, d)),
                  _const_spec(w_glu.shape)],
        out_specs=_row_spec(rows, d),
        out_shape=jax.ShapeDtypeStruct(x.shape, F32),
        scratch_shapes=[pltpu.VMEM((npack, bsz, 2 * S5_PACK_STATES), F32),
                        pltpu.VMEM((rows, d), F32),
                        pltpu.VMEM((rows, 2 * S5_PACK_STATES), F32),
                        pltpu.VMEM((rows, d), F32)],
        compiler_params=_params("arbitrary"),
        name="s5_layer",
    )(x, g0, sh, sc, gate, g1, wb, lbr, lbi, wcr, wci, d_skip.reshape(1, d), w_glu)


def _rope_kernel(pos_ref, inv_ref, cos_ref, sin_ref):
    ang = pos_ref[...].astype(F32) * inv_ref[...]
    lane = lax.broadcasted_iota(jnp.int32, ang.shape, 1)
    first_half = lane % DA_HEAD_DIM < DA_HEAD_DIM // 2
    cos_ref[...] = jnp.cos(ang)
    sin_ref[...] = jnp.where(first_half, -1.0, 1.0) * jnp.sin(ang)


def _rope_tables(pos_rows):
    n = pos_rows.shape[0]
    half = DA_HEAD_DIM // 2
    inv = ROPE_THETA ** (-jnp.arange(half, dtype=F32) / half)
    inv = jnp.tile(inv, LANES // half).reshape(1, LANES)
    tm = min(n, ROW_TILE)
    return pl.pallas_call(
        _rope_kernel,
        grid=(n // tm,),
        in_specs=[_row_spec(tm, 1), pl.BlockSpec((1, LANES), lambda i: (0, 0))],
        out_specs=[_row_spec(tm, LANES)] * 2,
        out_shape=[jax.ShapeDtypeStruct((n, LANES), F32)] * 2,
        compiler_params=_params("parallel"),
        name="rope_tables",
    )(pos_rows, inv)


def _qkv_kernel(x_ref, g_ref, sh_ref, sc_ref, w_ref, cos_ref, sin_ref, q_ref, k_ref, v_ref,
                *tmp_refs):
    bsz, tt, d = q_ref.shape
    h = _modulated_norm(x_ref[...], g_ref[...], sh_ref[...], sc_ref[...]).astype(BF16)
    cos, sin = cos_ref[...], sin_ref[...]
    lane = lax.broadcasted_iota(jnp.int32, cos.shape, 1)
    half = DA_HEAD_DIM // 2
    first_half = lane % DA_HEAD_DIM < half

    def emit(dst, j, val):
        tmp_refs[j][...] = val
        for b in range(bsz):
            dst[b, :, j * LANES:(j + 1) * LANES] = tmp_refs[j][pl.ds(b, tt, stride=bsz), :].astype(BF16)

    for src, dst, scale in ((0, q_ref, DA_HEAD_DIM ** -0.5), (d, k_ref, 1.0)):
        t = _dot(h, w_ref[:, src:src + d])
        for j in range(d // LANES):
            tj = t[:, j * LANES:(j + 1) * LANES]
            rot = jnp.where(first_half, pltpu.roll(tj, LANES - half, 1), pltpu.roll(tj, half, 1))
            emit(dst, j, (tj * cos + rot * sin) * scale)
    t = _dot(h, w_ref[:, 2 * d:])
    for j in range(d // LANES):
        emit(v_ref, j, t[:, j * LANES:(j + 1) * LANES])


def _qkv_proj(x, g, sh, sc, w_qkv, cos, sin):
    n, d = x.shape
    bsz = sh.shape[0]
    tm = min(n, ROW_TILE)
    batch_major = pl.BlockSpec((bsz, tm // bsz, d), lambda i: (0, i, 0))
    return pl.pallas_call(
        _qkv_kernel,
        grid=(n // tm,),
        in_specs=[_row_spec(tm, d), _const_spec((1, d)), _const_spec((bsz, d)),
                  _const_spec((bsz, d)), _const_spec(w_qkv.shape), _row_spec(tm, LANES),
                  _row_spec(tm, LANES)],
        out_specs=[batch_major] * 3,
        out_shape=[jax.ShapeDtypeStruct((bsz, n // bsz, d), BF16)] * 3,
        scratch_shapes=[pltpu.VMEM((tm, LANES), F32)] * (d // LANES),
        compiler_params=_params("parallel"),
        name="qkv_proj",
    )(x, g, sh, sc, w_qkv, cos, sin)


def _attn_kernel(q_ref, k_ref, v_ref, lq1_ref, lk1_ref, lq2_ref, lk2_ref, sg_ref, o_ref,
                 kt_ref, vext_ref, qs_ref, m_ref, acc_ref, s_refs, p_refs, a_refs, *,
                 lambda_init, tq):
    seq = q_ref.shape[0]
    vext_ref[:, :DA_V_DIM] = v_ref[...]
    col = lax.broadcasted_iota(jnp.int32, (seq, DA_V_DIM), 1)
    vext_ref[:, DA_V_DIM:] = jnp.where(col == 0, 1.0, 0.0).astype(BF16)
    tb = min(seq, ROW_TILE)
    for blk in range(seq // tb):
        kt_ref[:, blk * tb:(blk + 1) * tb] = (
            k_ref[blk * tb:(blk + 1) * tb, :].astype(F32).T.astype(BF16))

    lam = (jnp.exp(jnp.sum(lq1_ref[...] * lk1_ref[...], keepdims=True))
           - jnp.exp(jnp.sum(lq2_ref[...] * lk2_ref[...], keepdims=True)) + lambda_init)

    lane = lax.broadcasted_iota(jnp.int32, (tq, 2 * DA_HEAD_DIM), 1)
    first = lane < DA_HEAD_DIM
    rows = 2 * tq
    tk = tq // 2

    def scores(t, slot):
        kt = kt_ref[:, pl.ds(pl.multiple_of(t * tk, tk), tk)]
        for i in range(rows // ATTN_MM_ROWS):
            rs = pl.ds(i * ATTN_MM_ROWS, ATTN_MM_ROWS)
            s_refs[slot][rs, :] = _dot(qs_ref[rs, :], kt)

    def softmax(slot, diag):
        for i in range(rows // ATTN_SM_ROWS):
            rs = pl.ds(i * ATTN_SM_ROWS, ATTN_SM_ROWS)
            s = s_refs[slot][rs, :]
            if diag is not None:
                qrow = (i * ATTN_SM_ROWS) % tq + lax.broadcasted_iota(jnp.int32, s.shape, 0)
                key = diag * tk + lax.broadcasted_iota(jnp.int32, s.shape, 1)
                s = jnp.where(key < (qrow // CHUNK + 1) * CHUNK, s, NEG_BIG)
            parts = [s[:, j * LANES:(j + 1) * LANES] for j in range(tk // LANES)]
            m_prev = m_ref[rs, :]
            m_new = jnp.maximum(
                m_prev, jnp.max(functools.reduce(jnp.maximum, parts), axis=-1, keepdims=True))
            p_refs[slot][rs, :] = jnp.concatenate(
                [jnp.exp(pj - m_new) for pj in parts], axis=1).astype(BF16)
            a_refs[slot][rs, :] = jnp.exp(m_prev - m_new)
            m_ref[rs, :] = m_new

    def values(t, slot):
        vb = vext_ref[pl.ds(pl.multiple_of(t * tk, tk), tk), :]
        for i in range(rows // ATTN_MM_ROWS):
            rs = pl.ds(i * ATTN_MM_ROWS, ATTN_MM_ROWS)
            alpha = a_refs[slot][rs, :]
            acc_ref[rs, :] = (jnp.concatenate([alpha, alpha], axis=1) * acc_ref[rs, :]
                              + _dot(p_refs[slot][rs, :], vb))

    def q_block(qi, carry):
        q0 = pl.multiple_of(qi * tq, tq)
        qb = q_ref[pl.ds(q0, tq), :]
        zero = jnp.zeros_like(qb)
        qs_ref[:tq, :] = jnp.where(first, qb, zero)
        qs_ref[tq:, :] = jnp.where(first, zero, qb)
        acc_ref[...] = jnp.zeros_like(acc_ref)
        m_ref[...] = jnp.full(m_ref.shape, NEG_BIG, F32)
        p_refs[1][...] = jnp.zeros_like(p_refs[1])
        a_refs[1][...] = jnp.ones_like(a_refs[1])

        def pair(u, last):
            t = 2 * u
            scores(t + 1, 1)
            softmax(0, 0 if last else None)
            values(jnp.maximum(t - 1, 0), 1)
            if not last:
                scores(t + 2, 0)
            softmax(1, 1 if last else None)
            values(t, 0)

        def full_pair(u, carry):
            pair(u, False)
            return carry

        scores(0, 0)
        lax.fori_loop(0, qi, full_pair, 0)
        pair(qi, True)
        values(2 * qi + 1, 1)

        for i in range(tq // ATTN_SM_ROWS):
            o = []
            for comp in range(2):
                acc = acc_ref[pl.ds(comp * tq + i * ATTN_SM_ROWS, ATTN_SM_ROWS), :]
                o.append(acc[:, :DA_V_DIM] / acc[:, DA_V_DIM:DA_V_DIM + 1])
            od = _rms(o[0] - lam * o[1], sg_ref[...]) * (1.0 - lambda_init)
            o_ref[pl.ds(q0 + i * ATTN_SM_ROWS, ATTN_SM_ROWS), :] = od.astype(BF16)
        return carry

    lax.fori_loop(0, seq // tq, q_block, 0)


def _diff_attention(q, k, v, lq1, lk1, lq2, lk2, subln_g, lambda_init):
    bsz, seq, d = q.shape
    tq = min(seq, ATTN_TQ)
    rows, tk = 2 * tq, tq // 2
    head = pl.BlockSpec((None, seq, DA_V_DIM), lambda b, h: (b, 0, h))
    vec = _const_spec((1, DA_HEAD_DIM))
    return pl.pallas_call(
        functools.partial(_attn_kernel, lambda_init=lambda_init, tq=tq),
        grid=(bsz, d // DA_V_DIM),
        in_specs=[head, head, head, vec, vec, vec, vec, _const_spec((1, DA_V_DIM))],
        out_specs=head,
        out_shape=jax.ShapeDtypeStruct((bsz, seq, d), BF16),
        scratch_shapes=[pltpu.VMEM((2 * DA_HEAD_DIM, seq), BF16),
                        pltpu.VMEM((seq, 2 * DA_V_DIM), BF16),
                        pltpu.VMEM((rows, 2 * DA_HEAD_DIM), BF16),
                        pltpu.VMEM((rows, LANES), F32),
                        pltpu.VMEM((rows, 2 * DA_V_DIM), F32),
                        [pltpu.VMEM((rows, tk), F32)] * 2,
                        [pltpu.VMEM((rows, tk), BF16)] * 2,
                        [pltpu.VMEM((rows, LANES), F32)] * 2],
        compiler_params=_params("parallel", "parallel"),
        name="diff_attention",
    )(q, k, v, lq1.reshape(1, -1), lk1.reshape(1, -1), lq2.reshape(1, -1), lk2.reshape(1, -1),
      subln_g.reshape(1, -1))


def _oproj_kernel(o_ref, x_ref, w_ref, g_ref, gate_ref, out_ref, *tmp_refs):
    bsz, tt, d = o_ref.shape
    for j in range(d // LANES):
        for b in range(bsz):
            tmp_refs[j][pl.ds(b, tt, stride=bsz), :] = o_ref[b, :, j * LANES:(j + 1) * LANES].astype(F32)
    o = jnp.concatenate([t[...] for t in tmp_refs], axis=1).astype(BF16)
    out_ref[...] = _gated_residual(x_ref[...], _dot(o, w_ref[...]), g_ref[...], gate_ref[...])


def _out_proj(o, x, w_o, g, gate):
    n, d = x.shape
    bsz = gate.shape[0]
    tm = min(n, ROW_TILE)
    return pl.pallas_call(
        _oproj_kernel,
        grid=(n // tm,),
        in_specs=[pl.BlockSpec((bsz, tm // bsz, d), lambda i: (0, i, 0)), _row_spec(tm, d),
                  _const_spec(w_o.shape), _const_spec((1, d)), _const_spec((bsz, d))],
        out_specs=_row_spec(tm, d),
        out_shape=jax.ShapeDtypeStruct(x.shape, F32),
        scratch_shapes=[pltpu.VMEM((tm, LANES), F32)] * (d // LANES),
        compiler_params=_params("parallel"),
        name="attn_out_proj",
    )(o, x, w_o, g, gate)


def _ffn_kernel(x_ref, xprev_ref, g2_ref, sh_ref, sc_ref, gate_ref, g3_ref, win_ref, cw_ref, cb_ref,
                wout_ref, out_ref, h_ref, acc_ref):
    tm, d = x_ref.shape
    f = wout_ref.shape[0]
    bsz = sh_ref.shape[0]
    halo = xprev_ref.shape[0]
    g2, sh, sc = g2_ref[...], sh_ref[...], sc_ref[...]
    h_ref[:halo, :] = _modulated_norm(xprev_ref[...], g2, sh, sc).astype(BF16)
    h_ref[halo:, :] = _modulated_norm(x_ref[...], g2, sh, sc).astype(BF16)
    row = lax.broadcasted_iota(jnp.int32, (tm + halo, FFN_TF), 0)
    keep = jnp.logical_or(row >= halo, pl.program_id(0) > 0)
    acc_ref[...] = jnp.zeros_like(acc_ref)

    def conv(u, col):
        u = jnp.where(keep, u, 0.0)
        w = cw_ref[:, col]
        out = cb_ref[:, col]
        for j in range(CONV_WIDTH):
            out = out + w[j:j + 1] * u[j * bsz:j * bsz + tm]
        return out

    def body(i, carry):
        c0 = pl.multiple_of(i * FFN_TF, FFN_TF)
        ca, cb = pl.ds(c0, FFN_TF), pl.ds(f + c0, FFN_TF)
        hb = h_ref[...]
        a = conv(_dot(hb, win_ref[:, ca]), ca)
        b = conv(_dot(hb, win_ref[:, cb]), cb)
        acc_ref[...] += _dot((jax.nn.gelu(a) * b).astype(BF16), wout_ref[ca, :])
        return carry

    lax.fori_loop(0, f // FFN_TF, body, 0)
    out_ref[...] = _gated_residual(x_ref[...], acc_ref[...], g3_ref[...], gate_ref[...])


def _conv_ffn(x, g2, sh, sc, gate, g3, w_in, conv_w, conv_b, w_out):
    n, d = x.shape
    f = w_out.shape[0]
    bsz = sh.shape[0]
    tm = min(n, ROW_TILE)
    halo = (CONV_WIDTH - 1) * bsz
    prev = pl.BlockSpec((halo, d), lambda i: (jnp.maximum(i * (tm // halo) - 1, 0), 0))
    vec = _const_spec((bsz, d))
    return pl.pallas_call(
        _ffn_kernel,
        grid=(n // tm,),
        in_specs=[_row_spec(tm, d), prev, _const_spec((1, d)), vec, vec, vec, _const_spec((1, d)),
                  _const_spec(w_in.shape), _const_spec(conv_w.shape), _const_spec((1, 2 * f)),
                  _const_spec(w_out.shape)],
        out_specs=_row_spec(tm, d),
        out_shape=jax.ShapeDtypeStruct(x.shape, F32),
        scratch_shapes=[pltpu.VMEM((tm + halo, d), BF16), pltpu.VMEM((tm, d), F32)],
        compiler_params=_params("parallel"),
        name="conv_ffn",
    )(x, x, g2, sh, sc, gate, g3, w_in, conv_w, conv_b.reshape(1, 2 * f), w_out)


def kernel(x, c, positions, ada_w, ada_b, norm_g, s5_a_re, s5_a_im, s5_log_dt, s5_b_re, s5_b_im,
           s5_c_re, s5_c_im, s5_d, s5_w_glu, da_w_qkv, da_w_o, da_lq1, da_lk1, da_lq2, da_lk2,
           da_subln_g, ffn_w_in, ffn_conv_w, ffn_conv_b, ffn_w_out):
    depth = ada_w.shape[0]
    bsz, seq, d = x.shape
    n = seq * bsz
    mod = _ada_mod(c, ada_w, ada_b).reshape(depth, bsz, 6, d)
    lb_re, lb_im, bb_re, bb_im = _s5_discretise(s5_a_re, s5_a_im, s5_log_dt, s5_b_re, s5_b_im)
    cos, sin = _rope_tables(positions.T.reshape(n, 1))
    x = jnp.swapaxes(x, 0, 1).reshape(n, d)
    for i in range(depth):
        sh_t, sc_t, g_t, sh_c, sc_c, g_c = (mod[i, :, m] for m in range(6))
        gains = norm_g[i].reshape(4, 1, d)
        j = i // 2
        if i % 2 == 0:
            packed = _s5_pack_weights(lb_re[j], lb_im[j], bb_re[j], bb_im[j], s5_c_re[j], s5_c_im[j])
            x = _s5_layer(x, gains[0], sh_t, sc_t, g_t, gains[1], packed, s5_d[j],
                          s5_w_glu[j].astype(BF16))
        else:
            lambda_init = 0.8 - 0.6 * math.exp(-0.3 * i)
            q, k, v = _qkv_proj(x, gains[0], sh_t, sc_t, da_w_qkv[j].astype(BF16), cos, sin)
            o = _diff_attention(q, k, v, da_lq1[j], da_lk1[j], da_lq2[j], da_lk2[j],
                                da_subln_g[j], lambda_init)
            x = _out_proj(o, x, da_w_o[j].astype(BF16), gains[1], g_t)
        x = _conv_ffn(x, gains[2], sh_c, sc_c, g_c, gains[3], ffn_w_in[i].astype(BF16),
                      ffn_conv_w[i], ffn_conv_b[i], ffn_w_out[i].astype(BF16))
    return jnp.swapaxes(x.reshape(seq, bsz, d), 0, 1)
```

```python
import functools
import math

import jax
import jax.numpy as jnp
from jax import lax
from jax.experimental import pallas as pl
from jax.experimental.pallas import tpu as pltpu

F32 = jnp.float32
BF16 = jnp.bfloat16

EPS = 1e-6
CHUNK = 64
S5_GROUP = 16
S5_STATE = 64
S5_LAMBDA_RE_MAX = -1e-4
DA_HEADS = 8
DA_HEAD_DIM = 64
DA_V_DIM = 2 * DA_HEAD_DIM
ROPE_THETA = 10000.0
CONV_WIDTH = 3

LANES = 128
MXU_DIM = 256
VMEM_LIMIT_BYTES = 56 * 1024 * 1024

S5_PACK_GROUPS = MXU_DIM // S5_GROUP
S5_PACK_STATES = S5_PACK_GROUPS * S5_STATE
S5_SCAN_LANES = 4 * LANES

ROW_TILE = 512
ATTN_TQ = 512
ATTN_MM_ROWS = 256
ATTN_SM_ROWS = 128
FFN_TF = 256
FFN_ROW_SLICES = 4
NEG_BIG = -1e30


def _params(*sem):
    return pltpu.CompilerParams(dimension_semantics=sem, vmem_limit_bytes=VMEM_LIMIT_BYTES)


def _const_spec(shape):
    nd = len(shape)
    return pl.BlockSpec(shape, lambda *_: (0,) * nd, pipeline_mode=pl.Buffered(1))


def _row_spec(tm, width):
    return pl.BlockSpec((tm, width), lambda i: (i, 0))


def _rms(x, g):
    ms = jnp.mean(x * x, axis=-1, keepdims=True)
    return x * lax.rsqrt(ms + EPS) * g


def _per_seq(x, bsz, fn):
    rows, d = x.shape
    if bsz == 1:
        return fn(x)
    return fn(x.reshape(rows // bsz, bsz, d)).reshape(rows, d)


def _modulated_norm(x, g, sh, sc):
    return _per_seq(_rms(x, g), sh.shape[0], lambda y: y * (1.0 + sc) + sh)


def _gated_residual(x, y, g, gate):
    return x + _per_seq(_rms(y, g), gate.shape[0], lambda r: gate * r)


def _dot(a, b):
    return jnp.dot(a, b, preferred_element_type=F32)


def _ada_kernel(c_ref, w_ref, b_ref, o_ref):
    c = c_ref[...]
    cond = (c * jax.nn.sigmoid(c)).astype(BF16)
    o_ref[...] = _dot(cond, w_ref[...].astype(BF16)) + b_ref[...]


def _ada_mod(c, ada_w, ada_b):
    depth, d, n = ada_w.shape
    bsz = c.shape[0]
    tn = n // 4
    return pl.pallas_call(
        _ada_kernel,
        grid=(depth, n // tn),
        in_specs=[
            pl.BlockSpec((bsz, d), lambda i, j: (0, 0)),
            pl.BlockSpec((None, d, tn), lambda i, j: (i, 0, j)),
            pl.BlockSpec((None, 1, tn), lambda i, j: (i, 0, j)),
        ],
        out_specs=pl.BlockSpec((None, bsz, tn), lambda i, j: (i, 0, j)),
        out_shape=jax.ShapeDtypeStruct((depth, bsz, n), F32),
        compiler_params=_params("parallel", "parallel"),
        name="ada_mod",
    )(c, ada_w, ada_b.reshape(depth, 1, n))


def _s5_disc_kernel(are_ref, aim_ref, ldt_ref, br_ref, bi_ref, lbr_ref, lbi_ref, bbr_ref, bbi_ref):
    lam_re = jnp.minimum(are_ref[...], S5_LAMBDA_RE_MAX)
    lam_im = aim_ref[...]
    dt = jnp.exp(ldt_ref[...])
    dre, dimg = lam_re * dt, lam_im * dt
    mag = jnp.exp(dre)
    lb_re, lb_im = mag * jnp.cos(dimg), mag * jnp.sin(dimg)
    den = lam_re * lam_re + lam_im * lam_im
    nr = lb_re - 1.0
    f_re = (nr * lam_re + lb_im * lam_im) / den
    f_im = (lb_im * lam_re - nr * lam_im) / den
    br, bi = br_ref[...], bi_ref[...]
    lbr_ref[...] = lb_re
    lbi_ref[...] = lb_im
    bbr_ref[...] = f_re * br - f_im * bi
    bbi_ref[...] = f_re * bi + f_im * br


def _s5_discretise(a_re, a_im, log_dt, b_re, b_im):
    n, g, p = a_re.shape
    cg = b_re.shape[-1]
    vec = pl.BlockSpec((None, g, 1, p), lambda i: (i, 0, 0, 0))
    mat = pl.BlockSpec((None, g, cg, p), lambda i: (i, 0, 0, 0))
    return pl.pallas_call(
        _s5_disc_kernel,
        grid=(n,),
        in_specs=[vec, vec, pl.BlockSpec((None, g, 1, 1), lambda i: (i, 0, 0, 0)), mat, mat],
        out_specs=[vec, vec, mat, mat],
        out_shape=[jax.ShapeDtypeStruct((n, g, 1, p), F32)] * 2
        + [jax.ShapeDtypeStruct((n, g, cg, p), F32)] * 2,
        compiler_params=_params("parallel"),
        name="s5_discretise",
    )(a_re.reshape(n, g, 1, p), a_im.reshape(n, g, 1, p), log_dt.reshape(n, g, 1, 1),
      jnp.swapaxes(b_re, -1, -2), jnp.swapaxes(b_im, -1, -2))


def _s5_pack_weights(lb_re, lb_im, bb_re, bb_im, c_re, c_im):
    g, cg, p = bb_re.shape
    gp = S5_PACK_GROUPS
    npack = g // gp
    eye = jnp.eye(gp, dtype=F32)

    def in_proj(bb):
        return jnp.einsum("kgcp,gh->kgchp", bb.reshape(npack, gp, cg, p), eye).reshape(
            npack, gp * cg, gp * p)

    def out_proj(c):
        return jnp.einsum("kgcp,gh->kgphc", c.reshape(npack, gp, cg, p), eye).reshape(
            npack, gp * p, gp * cg)

    wb = jnp.concatenate([in_proj(bb_re), in_proj(bb_im)], axis=-1).astype(BF16)
    return (wb, out_proj(c_re).astype(BF16), out_proj(c_im).astype(BF16),
            lb_re.reshape(npack, 1, gp * p), lb_im.reshape(npack, 1, gp * p))


def _s5_kernel(x_ref, g0_ref, sh_ref, sc_ref, gate_ref, g1_ref, wb_ref, lbr_ref, lbi_ref,
               wcr_ref, wci_ref, dsk_ref, wglu_ref, out_ref, h_ref, u_ref, bu_refs, y_ref):
    rows, d = x_ref.shape
    bsz = sh_ref.shape[0]
    npack = wb_ref.shape[0]
    ns = S5_PACK_STATES

    @pl.when(pl.program_id(0) == 0)
    def _():
        h_ref[...] = jnp.zeros_like(h_ref)

    u_ref[...] = _modulated_norm(x_ref[...], g0_ref[...], sh_ref[...], sc_ref[...])

    for k in range(npack):
        c0 = k * MXU_DIM
        bu_ref = bu_refs[k % 2]
        bu_ref[...] = _dot(u_ref[:, c0:c0 + MXU_DIM].astype(BF16), wb_ref[k])
        for s in range(ns // S5_SCAN_LANES):
            re = pl.ds(s * S5_SCAN_LANES, S5_SCAN_LANES)
            im = pl.ds(ns + s * S5_SCAN_LANES, S5_SCAN_LANES)
            lbr = jnp.broadcast_to(lbr_ref[k, :, re], (bsz, S5_SCAN_LANES))
            lbi = jnp.broadcast_to(lbi_ref[k, :, re], (bsz, S5_SCAN_LANES))
            hr, hi = h_ref[k, :, re], h_ref[k, :, im]
            for t in range(rows // bsz):
                frame = pl.ds(t * bsz, bsz)
                hr, hi = (lbr * hr - lbi * hi + bu_ref[frame, re],
                          lbr * hi + lbi * hr + bu_ref[frame, im])
                bu_ref[frame, re] = hr
                bu_ref[frame, im] = hi
            h_ref[k, :, re] = hr
            h_ref[k, :, im] = hi
        y_ref[:, c0:c0 + MXU_DIM] = (_dot(bu_ref[:, :ns].astype(BF16), wcr_ref[k])
                                     - _dot(bu_ref[:, ns:].astype(BF16), wci_ref[k]))

    y = y_ref[...] + dsk_ref[...] * u_ref[...]
    o = _dot(jax.nn.gelu(y).astype(BF16), wglu_ref[...])
    mix = o[:, :d] * jax.nn.sigmoid(o[:, d:])
    out_ref[...] = _gated_residual(x_ref[...], mix, g1_ref[...], gate_ref[...])


def _s5_layer(x, g0, sh, sc, gate, g1, packed, d_skip, w_glu):
    n, d = x.shape
    bsz = sh.shape[0]
    wb, wcr, wci, lbr, lbi = packed
    npack = wb.shape[0]
    rows = CHUNK * bsz
    return pl.pallas_call(
        _s5_kernel,
        grid=(n // rows,),
        in_specs=[_row_spec(rows, d), _const_spec((1, d)), _const_spec((bsz, d)),
                  _const_spec((bsz, d)), _const_spec((bsz, d)), _const_spec((1, d)),
                  _const_spec(wb.shape), _const_spec(lbr.shape), _const_spec(lbi.shape),
                  _const_spec(wcr.shape), _const_spec(wci.shape), _const_spec((1, d)),
                  _const_spec(w_glu.shape)],
        out_specs=_row_spec(rows, d),
        out_shape=jax.ShapeDtypeStruct(x.shape, F32),
        scratch_shapes=[pltpu.VMEM((npack, bsz, 2 * S5_PACK_STATES), F32),
                        pltpu.VMEM((rows, d), F32),
                        [pltpu.VMEM((rows, 2 * S5_PACK_STATES), F32)] * 2,
                        pltpu.VMEM((rows, d), F32)],
        compiler_params=_params("arbitrary"),
        name="s5_layer",
    )(x, g0, sh, sc, gate, g1, wb, lbr, lbi, wcr, wci, d_skip.reshape(1, d), w_glu)


def _rope_kernel(pos_ref, inv_ref, cos_ref, sin_ref):
    ang = pos_ref[...].astype(F32) * inv_ref[...]
    lane = lax.broadcasted_iota(jnp.int32, ang.shape, 1)
    first_half = lane % DA_HEAD_DIM < DA_HEAD_DIM // 2
    cos_ref[...] = jnp.cos(ang)
    sin_ref[...] = jnp.where(first_half, -1.0, 1.0) * jnp.sin(ang)


def _rope_tables(pos_rows):
    n = pos_rows.shape[0]
    half = DA_HEAD_DIM // 2
    inv = ROPE_THETA ** (-jnp.arange(half, dtype=F32) / half)
    inv = jnp.tile(inv, LANES // half).reshape(1, LANES)
    tm = min(n, ROW_TILE)
    return pl.pallas_call(
        _rope_kernel,
        grid=(n // tm,),
        in_specs=[_row_spec(tm, 1), pl.BlockSpec((1, LANES), lambda i: (0, 0))],
        out_specs=[_row_spec(tm, LANES)] * 2,
        out_shape=[jax.ShapeDtypeStruct((n, LANES), F32)] * 2,
        compiler_params=_params("parallel"),
        name="rope_tables",
    )(pos_rows, inv)


def _qkv_kernel(x_ref, g_ref, sh_ref, sc_ref, w_ref, cos_ref, sin_ref, q_ref, k_ref, v_ref,
                *tmp_refs):
    bsz, tt, d = q_ref.shape
    h = _modulated_norm(x_ref[...], g_ref[...], sh_ref[...], sc_ref[...]).astype(BF16)
    cos, sin = cos_ref[...], sin_ref[...]
    lane = lax.broadcasted_iota(jnp.int32, cos.shape, 1)
    half = DA_HEAD_DIM // 2
    first_half = lane % DA_HEAD_DIM < half

    def emit(dst, j, val):
        tmp_refs[j][...] = val
        for b in range(bsz):
            dst[b, :, j * LANES:(j + 1) * LANES] = tmp_refs[j][pl.ds(b, tt, stride=bsz), :].astype(BF16)

    for src, dst, scale in ((0, q_ref, DA_HEAD_DIM ** -0.5), (d, k_ref, 1.0)):
        t = _dot(h, w_ref[:, src:src + d])
        for j in range(d // LANES):
            tj = t[:, j * LANES:(j + 1) * LANES]
            rot = jnp.where(first_half, pltpu.roll(tj, LANES - half, 1), pltpu.roll(tj, half, 1))
            emit(dst, j, (tj * cos + rot * sin) * scale)
    t = _dot(h, w_ref[:, 2 * d:])
    for j in range(d // LANES):
        emit(v_ref, j, t[:, j * LANES:(j + 1) * LANES])


def _qkv_proj(x, g, sh, sc, w_qkv, cos, sin):
    n, d = x.shape
    bsz = sh.shape[0]
    tm = min(n, ROW_TILE)
    batch_major = pl.BlockSpec((bsz, tm // bsz, d), lambda i: (0, i, 0))
    return pl.pallas_call(
        _qkv_kernel,
        grid=(n // tm,),
        in_specs=[_row_spec(tm, d), _const_spec((1, d)), _const_spec((bsz, d)),
                  _const_spec((bsz, d)), _const_spec(w_qkv.shape), _row_spec(tm, LANES),
                  _row_spec(tm, LANES)],
        out_specs=[batch_major] * 3,
        out_shape=[jax.ShapeDtypeStruct((bsz, n // bsz, d), BF16)] * 3,
        scratch_shapes=[pltpu.VMEM((tm, LANES), F32)] * (d // LANES),
        compiler_params=_params("parallel"),
        name="qkv_proj",
    )(x, g, sh, sc, w_qkv, cos, sin)


def _attn_kernel(q_ref, k_ref, v_ref, lq1_ref, lk1_ref, lq2_ref, lk2_ref, sg_ref, o_ref,
                 kt_ref, vext_ref, qs_ref, m_ref, acc_ref, s_refs, p_refs, a_refs, *,
                 lambda_init, tq):
    seq = q_ref.shape[0]
    vext_ref[:, :DA_V_DIM] = v_ref[...]
    col = lax.broadcasted_iota(jnp.int32, (seq, DA_V_DIM), 1)
    vext_ref[:, DA_V_DIM:] = jnp.where(col == 0, 1.0, 0.0).astype(BF16)
    tb = min(seq, ROW_TILE)
    for blk in range(seq // tb):
        kt_ref[:, blk * tb:(blk + 1) * tb] = (
            k_ref[blk * tb:(blk + 1) * tb, :].astype(F32).T.astype(BF16))

    lam = (jnp.exp(jnp.sum(lq1_ref[...] * lk1_ref[...], keepdims=True))
           - jnp.exp(jnp.sum(lq2_ref[...] * lk2_ref[...], keepdims=True)) + lambda_init)

    lane = lax.broadcasted_iota(jnp.int32, (tq, 2 * DA_HEAD_DIM), 1)
    first = lane < DA_HEAD_DIM
    rows = 2 * tq
    tk = tq // 2

    def scores(t, slot):
        kt = kt_ref[:, pl.ds(pl.multiple_of(t * tk, tk), tk)]
        for i in range(rows // ATTN_MM_ROWS):
            rs = pl.ds(i * ATTN_MM_ROWS, ATTN_MM_ROWS)
            s_refs[slot][rs, :] = _dot(qs_ref[rs, :], kt)

    def softmax(slot, diag):
        for i in range(rows // ATTN_SM_ROWS):
            rs = pl.ds(i * ATTN_SM_ROWS, ATTN_SM_ROWS)
            s = s_refs[slot][rs, :]
            if diag is not None:
                qrow = (i * ATTN_SM_ROWS) % tq + lax.broadcasted_iota(jnp.int32, s.shape, 0)
                key = diag * tk + lax.broadcasted_iota(jnp.int32, s.shape, 1)
                s = jnp.where(key < (qrow // CHUNK + 1) * CHUNK, s, NEG_BIG)
            parts = [s[:, j * LANES:(j + 1) * LANES] for j in range(tk // LANES)]
            m_prev = m_ref[rs, :]
            m_new = jnp.maximum(
                m_prev, jnp.max(functools.reduce(jnp.maximum, parts), axis=-1, keepdims=True))
            p_refs[slot][rs, :] = jnp.concatenate(
                [jnp.exp(pj - m_new) for pj in parts], axis=1).astype(BF16)
            a_refs[slot][rs, :] = jnp.exp(m_prev - m_new)
            m_ref[rs, :] = m_new

    def values(t, slot):
        vb = vext_ref[pl.ds(pl.multiple_of(t * tk, tk), tk), :]
        for i in range(rows // ATTN_MM_ROWS):
            rs = pl.ds(i * ATTN_MM_ROWS, ATTN_MM_ROWS)
            alpha = a_refs[slot][rs, :]
            acc_ref[rs, :] = (jnp.concatenate([alpha, alpha], axis=1) * acc_ref[rs, :]
                              + _dot(p_refs[slot][rs, :], vb))

    def q_block(qi, carry):
        q0 = pl.multiple_of(qi * tq, tq)
        qb = q_ref[pl.ds(q0, tq), :]
        zero = jnp.zeros_like(qb)
        qs_ref[:tq, :] = jnp.where(first, qb, zero)
        qs_ref[tq:, :] = jnp.where(first, zero, qb)
        acc_ref[...] = jnp.zeros_like(acc_ref)
        m_ref[...] = jnp.full(m_ref.shape, NEG_BIG, F32)
        p_refs[1][...] = jnp.zeros_like(p_refs[1])
        a_refs[1][...] = jnp.ones_like(a_refs[1])

        def pair(u, last):
            t = 2 * u
            scores(t + 1, 1)
            softmax(0, 0 if last else None)
            values(jnp.maximum(t - 1, 0), 1)
            if not last:
                scores(t + 2, 0)
            softmax(1, 1 if last else None)
            values(t, 0)

        def full_pair(u, carry):
            pair(u, False)
            return carry

        scores(0, 0)
        lax.fori_loop(0, qi, full_pair, 0)
        pair(qi, True)
        values(2 * qi + 1, 1)

        for i in range(tq // ATTN_SM_ROWS):
            o = []
            for comp in range(2):
                acc = acc_ref[pl.ds(comp * tq + i * ATTN_SM_ROWS, ATTN_SM_ROWS), :]
                o.append(acc[:, :DA_V_DIM] / acc[:, DA_V_DIM:DA_V_DIM + 1])
            od = _rms(o[0] - lam * o[1], sg_ref[...]) * (1.0 - lambda_init)
            o_ref[pl.ds(q0 + i * ATTN_SM_ROWS, ATTN_SM_ROWS), :] = od.astype(BF16)
        return carry

    lax.fori_loop(0, seq // tq, q_block, 0)


def _diff_attention(q, k, v, lq1, lk1, lq2, lk2, subln_g, lambda_init):
    bsz, seq, d = q.shape
    tq = min(seq, ATTN_TQ)
    rows, tk = 2 * tq, tq // 2
    head = pl.BlockSpec((None, seq, DA_V_DIM), lambda b, h: (b, 0, h))
    vec = _const_spec((1, DA_HEAD_DIM))
    return pl.pallas_call(
        functools.partial(_attn_kernel, lambda_init=lambda_init, tq=tq),
        grid=(bsz, d // DA_V_DIM),
        in_specs=[head, head, head, vec, vec, vec, vec, _const_spec((1, DA_V_DIM))],
        out_specs=head,
        out_shape=jax.ShapeDtypeStruct((bsz, seq, d), BF16),
        scratch_shapes=[pltpu.VMEM((2 * DA_HEAD_DIM, seq), BF16),
                        pltpu.VMEM((seq, 2 * DA_V_DIM), BF16),
                        pltpu.VMEM((rows, 2 * DA_HEAD_DIM), BF16),
                        pltpu.VMEM((rows, LANES), F32),
                        pltpu.VMEM((rows, 2 * DA_V_DIM), F32),
                        [pltpu.VMEM((rows, tk), F32)] * 2,
                        [pltpu.VMEM((rows, tk), BF16)] * 2,
                        [pltpu.VMEM((rows, LANES), F32)] * 2],
        compiler_params=_params("parallel", "parallel"),
        name="diff_attention",
    )(q, k, v, lq1.reshape(1, -1), lk1.reshape(1, -1), lq2.reshape(1, -1), lk2.reshape(1, -1),
      subln_g.reshape(1, -1))


def _oproj_kernel(o_ref, x_ref, w_ref, g_ref, gate_ref, out_ref, *tmp_refs):
    bsz, tt, d = o_ref.shape
    for j in range(d // LANES):
        for b in range(bsz):
            tmp_refs[j][pl.ds(b, tt, stride=bsz), :] = o_ref[b, :, j * LANES:(j + 1) * LANES].astype(F32)
    o = jnp.concatenate([t[...] for t in tmp_refs], axis=1).astype(BF16)
    out_ref[...] = _gated_residual(x_ref[...], _dot(o, w_ref[...]), g_ref[...], gate_ref[...])


def _out_proj(o, x, w_o, g, gate):
    n, d = x.shape
    bsz = gate.shape[0]
    tm = min(n, ROW_TILE)
    return pl.pallas_call(
        _oproj_kernel,
        grid=(n // tm,),
        in_specs=[pl.BlockSpec((bsz, tm // bsz, d), lambda i: (0, i, 0)), _row_spec(tm, d),
                  _const_spec(w_o.shape), _const_spec((1, d)), _const_spec((bsz, d))],
        out_specs=_row_spec(tm, d),
        out_shape=jax.ShapeDtypeStruct(x.shape, F32),
        scratch_shapes=[pltpu.VMEM((tm, LANES), F32)] * (d // LANES),
        compiler_params=_params("parallel"),
        name="attn_out_proj",
    )(o, x, w_o, g, gate)


def _ffn_kernel(x_ref, xprev_ref, g2_ref, sh_ref, sc_ref, gate_ref, g3_ref, win_ref, cw_ref, cb_ref,
                wout_ref, out_ref, h_ref, acc_ref, u_refs, g_refs):
    tm, d = x_ref.shape
    f = wout_ref.shape[0]
    bsz = sh_ref.shape[0]
    halo = xprev_ref.shape[0]
    tf = FFN_TF
    n_tiles = f // tf
    g2, sh, sc = g2_ref[...], sh_ref[...], sc_ref[...]
    h_ref[:halo, :] = _modulated_norm(xprev_ref[...], g2, sh, sc).astype(BF16)
    keep_halo = jnp.where(pl.program_id(0) > 0, 1.0, 0.0)
    acc_ref[...] = jnp.zeros_like(acc_ref)
    g_refs[1][...] = jnp.zeros_like(g_refs[1])

    step = tm // FFN_ROW_SLICES
    o_rows = [(k * step, (k + 1) * step) for k in range(FFN_ROW_SLICES)]
    u_rows = [(r0 + halo if k else 0, r1 + halo) for k, (r0, r1) in enumerate(o_rows)]

    def up(i, slot, s):
        c0 = pl.multiple_of(i * tf, tf)
        r0, r1 = u_rows[s]
        hb = h_ref[r0:r1, :]
        for half, col in enumerate((pl.ds(c0, tf), pl.ds(f + c0, tf))):
            u_refs[slot][r0:r1, half * tf:(half + 1) * tf] = _dot(hb, win_ref[:, col])
        if s == 0:
            u_refs[slot][:halo, :] = u_refs[slot][:halo, :] * keep_halo

    def gate(i, slot, s):
        c0 = pl.multiple_of(i * tf, tf)
        r0, r1 = o_rows[s]
        halves = []
        for half, col in enumerate((pl.ds(c0, tf), pl.ds(f + c0, tf))):
            w = cw_ref[:, col]
            out = cb_ref[:, col]
            for j in range(CONV_WIDTH):
                out = out + w[j:j + 1] * u_refs[slot][pl.ds(r0 + j * bsz, r1 - r0),
                                                      half * tf:(half + 1) * tf]
            halves.append(out)
        g_refs[slot][r0:r1, :] = (jax.nn.gelu(halves[0]) * halves[1]).astype(BF16)

    def down(i, slot, s):
        c0 = pl.multiple_of(i * tf, tf)
        r0, r1 = o_rows[s]
        acc_ref[r0:r1, :] += _dot(g_refs[slot][r0:r1, :], wout_ref[pl.ds(c0, tf), :])

    def stages(up_args, gate_args, down_args):
        for s in range(len(o_rows)):
            if up_args is not None:
                up(*up_args, s)
            if gate_args is not None:
                gate(*gate_args, s)
            if down_args is not None:
                down(*down_args, s)

    def pair(v, carry):
        i = 2 * v
        stages((i + 1, 1), (i, 0), (jnp.maximum(i - 1, 0), 1))
        stages((i + 2, 0), (i + 1, 1), (i, 0))
        return carry

    assert n_tiles % 2 == 1
    for s, (r0, r1) in enumerate(o_rows):
        h_ref[halo + r0:halo + r1, :] = _modulated_norm(x_ref[r0:r1, :], g2, sh, sc).astype(BF16)
        up(0, 0, s)
    lax.fori_loop(0, n_tiles // 2, pair, 0)
    stages(None, (n_tiles - 1, 0), (n_tiles - 2, 1))
    for s, (r0, r1) in enumerate(o_rows):
        down(n_tiles - 1, 0, s)
        out_ref[r0:r1, :] = _gated_residual(x_ref[r0:r1, :], acc_ref[r0:r1, :], g3_ref[...],
                                            gate_ref[...])


def _conv_ffn(x, g2, sh, sc, gate, g3, w_in, conv_w, conv_b, w_out):
    n, d = x.shape
    f = w_out.shape[0]
    bsz = sh.shape[0]
    tm = min(n, ROW_TILE)
    halo = (CONV_WIDTH - 1) * bsz
    prev = pl.BlockSpec((halo, d), lambda i: (jnp.maximum(i * (tm // halo) - 1, 0), 0))
    vec = _const_spec((bsz, d))
    return pl.pallas_call(
        _ffn_kernel,
        grid=(n // tm,),
        in_specs=[_row_spec(tm, d), prev, _const_spec((1, d)), vec, vec, vec, _const_spec((1, d)),
                  _const_spec(w_in.shape), _const_spec(conv_w.shape), _const_spec((1, 2 * f)),
                  _const_spec(w_out.shape)],
        out_specs=_row_spec(tm, d),
        out_shape=jax.ShapeDtypeStruct(x.shape, F32),
        scratch_shapes=[pltpu.VMEM((tm + halo, d), BF16),
                        pltpu.VMEM((tm, d), F32),
                        [pltpu.VMEM((tm + halo, 2 * FFN_TF), F32)] * 2,
                        [pltpu.VMEM((tm, FFN_TF), BF16)] * 2],
        compiler_params=_params("parallel"),
        name="conv_ffn",
    )(x, x, g2, sh, sc, gate, g3, w_in, conv_w, conv_b.reshape(1, 2 * f), w_out)


def kernel(x, c, positions, ada_w, ada_b, norm_g, s5_a_re, s5_a_im, s5_log_dt, s5_b_re, s5_b_im,
           s5_c_re, s5_c_im, s5_d, s5_w_glu, da_w_qkv, da_w_o, da_lq1, da_lk1, da_lq2, da_lk2,
           da_subln_g, ffn_w_in, ffn_conv_w, ffn_conv_b, ffn_w_out):
    depth = ada_w.shape[0]
    bsz, seq, d = x.shape
    n = seq * bsz
    mod = _ada_mod(c, ada_w, ada_b).reshape(depth, bsz, 6, d)
    lb_re, lb_im, bb_re, bb_im = _s5_discretise(s5_a_re, s5_a_im, s5_log_dt, s5_b_re, s5_b_im)
    cos, sin = _rope_tables(positions.T.reshape(n, 1))
    x = jnp.swapaxes(x, 0, 1).reshape(n, d)
    for i in range(depth):
        sh_t, sc_t, g_t, sh_c, sc_c, g_c = (mod[i, :, m] for m in range(6))
        gains = norm_g[i].reshape(4, 1, d)
        j = i // 2
        if i % 2 == 0:
            packed = _s5_pack_weights(lb_re[j], lb_im[j], bb_re[j], bb_im[j], s5_c_re[j], s5_c_im[j])
            x = _s5_layer(x, gains[0], sh_t, sc_t, g_t, gains[1], packed, s5_d[j],
                          s5_w_glu[j].astype(BF16))
        else:
            lambda_init = 0.8 - 0.6 * math.exp(-0.3 * i)
            q, k, v = _qkv_proj(x, gains[0], sh_t, sc_t, da_w_qkv[j].astype(BF16), cos, sin)
            o = _diff_attention(q, k, v, da_lq1[j], da_lk1[j], da_lq2[j], da_lk2[j],
                                da_subln_g[j], lambda_init)
            x = _out_proj(o, x, da_w_o[j].astype(BF16), gains[1], g_t)
        x = _conv_ffn(x, gains[2], sh_c, sc_c, g_c, gains[3], ffn_w_in[i].astype(BF16),
                      ffn_conv_w[i], ffn_conv_b[i], ffn_w_out[i].astype(BF16))
    return jnp.swapaxes(x.reshape(seq, bsz, d), 0, 1)
```

```python
import functools
import math

import jax
import jax.numpy as jnp
from jax import lax
from jax.experimental import pallas as pl
from jax.experimental.pallas import tpu as pltpu

F32 = jnp.float32
BF16 = jnp.bfloat16

EPS = 1e-6
CHUNK = 64
S5_GROUP = 16
S5_STATE = 64
S5_LAMBDA_RE_MAX = -1e-4
DA_HEADS = 8
DA_HEAD_DIM = 64
DA_V_DIM = 2 * DA_HEAD_DIM
ROPE_THETA = 10000.0
CONV_WIDTH = 3

LANES = 128
MXU_DIM = 256
VMEM_LIMIT_BYTES = 56 * 1024 * 1024

S5_PACK_GROUPS = MXU_DIM // S5_GROUP
S5_PACK_STATES = S5_PACK_GROUPS * S5_STATE
S5_SCAN_LANES = 4 * LANES

ROW_TILE = 1024
ATTN_TQ = 512
ATTN_SM_ROWS = 128
FFN_TF = 256
NEG_BIG = -1e30


def _params(*sem):
    return pltpu.CompilerParams(dimension_semantics=sem, vmem_limit_bytes=VMEM_LIMIT_BYTES)


def _const_spec(shape):
    nd = len(shape)
    return pl.BlockSpec(shape, lambda *_: (0,) * nd, pipeline_mode=pl.Buffered(1))


def _row_spec(tm, width):
    return pl.BlockSpec((tm, width), lambda i: (i, 0))


def _rms(x, g):
    ms = jnp.mean(x * x, axis=-1, keepdims=True)
    return x * lax.rsqrt(ms + EPS) * g


def _per_seq(x, bsz, fn):
    rows, d = x.shape
    if bsz == 1:
        return fn(x)
    return fn(x.reshape(rows // bsz, bsz, d)).reshape(rows, d)


def _modulated_norm(x, g, sh, sc):
    return _per_seq(_rms(x, g), sh.shape[0], lambda y: y * (1.0 + sc) + sh)


def _gated_residual(x, y, g, gate):
    return x + _per_seq(_rms(y, g), gate.shape[0], lambda r: gate * r)


def _dot(a, b):
    return jnp.dot(a, b, preferred_element_type=F32)


def _relayout_scratch(rows, d):
    return [pltpu.VMEM((rows, LANES), F32)] * (d // LANES)


def _put_batch_major(dst_ref, j, val, tmp_ref):
    bsz, tt, _ = dst_ref.shape
    tmp_ref[...] = val
    for b in range(bsz):
        dst_ref[b, :, j * LANES:(j + 1) * LANES] = (
            tmp_ref[pl.ds(b, tt, stride=bsz), :].astype(dst_ref.dtype))


def _get_time_major(src_ref, tmp_refs):
    bsz, tt, _ = src_ref.shape
    for j, tmp_ref in enumerate(tmp_refs):
        for b in range(bsz):
            tmp_ref[pl.ds(b, tt, stride=bsz), :] = src_ref[b, :, j * LANES:(j + 1) * LANES].astype(F32)
    return jnp.concatenate([t[...] for t in tmp_refs], axis=1)


def _ada_kernel(c_ref, w_ref, b_ref, o_ref):
    c = c_ref[...]
    cond = (c * jax.nn.sigmoid(c)).astype(BF16)
    o_ref[...] = _dot(cond, w_ref[...].astype(BF16)) + b_ref[...]


def _ada_mod(c, ada_w, ada_b):
    depth, d, n = ada_w.shape
    bsz = c.shape[0]
    tn = n // 4
    return pl.pallas_call(
        _ada_kernel,
        grid=(depth, n // tn),
        in_specs=[
            pl.BlockSpec((bsz, d), lambda i, j: (0, 0)),
            pl.BlockSpec((None, d, tn), lambda i, j: (i, 0, j)),
            pl.BlockSpec((None, 1, tn), lambda i, j: (i, 0, j)),
        ],
        out_specs=pl.BlockSpec((None, bsz, tn), lambda i, j: (i, 0, j)),
        out_shape=jax.ShapeDtypeStruct((depth, bsz, n), F32),
        compiler_params=_params("parallel", "parallel"),
        name="ada_mod",
    )(c, ada_w, ada_b.reshape(depth, 1, n))


def _s5_disc_kernel(are_ref, aim_ref, ldt_ref, br_ref, bi_ref, lbr_ref, lbi_ref, bbr_ref, bbi_ref):
    lam_re = jnp.minimum(are_ref[...], S5_LAMBDA_RE_MAX)
    lam_im = aim_ref[...]
    dt = jnp.exp(ldt_ref[...])
    dre, dimg = lam_re * dt, lam_im * dt
    mag = jnp.exp(dre)
    lb_re, lb_im = mag * jnp.cos(dimg), mag * jnp.sin(dimg)
    den = lam_re * lam_re + lam_im * lam_im
    nr = lb_re - 1.0
    f_re = (nr * lam_re + lb_im * lam_im) / den
    f_im = (lb_im * lam_re - nr * lam_im) / den
    br, bi = br_ref[...], bi_ref[...]
    lbr_ref[...] = lb_re
    lbi_ref[...] = lb_im
    bbr_ref[...] = f_re * br - f_im * bi
    bbi_ref[...] = f_re * bi + f_im * br


def _s5_discretise(a_re, a_im, log_dt, b_re, b_im):
    n, g, p = a_re.shape
    cg = b_re.shape[-1]
    vec = pl.BlockSpec((None, g, 1, p), lambda i: (i, 0, 0, 0))
    mat = pl.BlockSpec((None, g, cg, p), lambda i: (i, 0, 0, 0))
    return pl.pallas_call(
        _s5_disc_kernel,
        grid=(n,),
        in_specs=[vec, vec, pl.BlockSpec((None, g, 1, 1), lambda i: (i, 0, 0, 0)), mat, mat],
        out_specs=[vec, vec, mat, mat],
        out_shape=[jax.ShapeDtypeStruct((n, g, 1, p), F32)] * 2
        + [jax.ShapeDtypeStruct((n, g, cg, p), F32)] * 2,
        compiler_params=_params("parallel"),
        name="s5_discretise",
    )(a_re.reshape(n, g, 1, p), a_im.reshape(n, g, 1, p), log_dt.reshape(n, g, 1, 1),
      jnp.swapaxes(b_re, -1, -2), jnp.swapaxes(b_im, -1, -2))


def _s5_pack_weights(lb_re, lb_im, bb_re, bb_im, c_re, c_im):
    g, cg, p = bb_re.shape
    gp = S5_PACK_GROUPS
    npack = g // gp
    eye = jnp.eye(gp, dtype=F32)

    def in_proj(bb):
        return jnp.einsum("kgcp,gh->kgchp", bb.reshape(npack, gp, cg, p), eye).reshape(
            npack, gp * cg, gp * p)

    def out_proj(c):
        return jnp.einsum("kgcp,gh->kgphc", c.reshape(npack, gp, cg, p), eye).reshape(
            npack, gp * p, gp * cg)

    wb = jnp.concatenate([in_proj(bb_re), in_proj(bb_im)], axis=-1).astype(BF16)
    return (wb, out_proj(c_re).astype(BF16), out_proj(c_im).astype(BF16),
            lb_re.reshape(npack, 1, gp * p), lb_im.reshape(npack, 1, gp * p))


def _s5_kernel(x_ref, g0_ref, sh_ref, sc_ref, gate_ref, g1_ref, wb_ref, lbr_ref, lbi_ref,
               wcr_ref, wci_ref, dsk_ref, wglu_ref, out_ref, h_ref, u_ref, bu_refs, y_ref,
               *relayout):
    if relayout:
        xtm_ref, *tmp_refs = relayout
        xtm_ref[...] = _get_time_major(x_ref, tmp_refs)
        x_ref = xtm_ref
    rows, d = x_ref.shape
    bsz = sh_ref.shape[0]
    npack = wb_ref.shape[0]
    ns = S5_PACK_STATES

    @pl.when(pl.program_id(0) == 0)
    def _():
        h_ref[...] = jnp.zeros_like(h_ref)

    u_ref[...] = _modulated_norm(x_ref[...], g0_ref[...], sh_ref[...], sc_ref[...])

    for k in range(npack):
        c0 = k * MXU_DIM
        bu_ref = bu_refs[k % 2]
        bu_ref[...] = _dot(u_ref[:, c0:c0 + MXU_DIM].astype(BF16), wb_ref[k])
        for s in range(ns // S5_SCAN_LANES):
            re = pl.ds(s * S5_SCAN_LANES, S5_SCAN_LANES)
            im = pl.ds(ns + s * S5_SCAN_LANES, S5_SCAN_LANES)
            lbr = jnp.broadcast_to(lbr_ref[k, :, re], (bsz, S5_SCAN_LANES))
            lbi = jnp.broadcast_to(lbi_ref[k, :, re], (bsz, S5_SCAN_LANES))
            hr, hi = h_ref[k, :, re], h_ref[k, :, im]
            for t in range(rows // bsz):
                frame = pl.ds(t * bsz, bsz)
                hr, hi = (lbr * hr - lbi * hi + bu_ref[frame, re],
                          lbr * hi + lbi * hr + bu_ref[frame, im])
                bu_ref[frame, re] = hr
                bu_ref[frame, im] = hi
            h_ref[k, :, re] = hr
            h_ref[k, :, im] = hi
        y_ref[:, c0:c0 + MXU_DIM] = (_dot(bu_ref[:, :ns].astype(BF16), wcr_ref[k])
                                     - _dot(bu_ref[:, ns:].astype(BF16), wci_ref[k]))

    y = y_ref[...] + dsk_ref[...] * u_ref[...]
    o = _dot(jax.nn.gelu(y).astype(BF16), wglu_ref[...])
    mix = o[:, :d] * jax.nn.sigmoid(o[:, d:])
    out_ref[...] = _gated_residual(x_ref[...], mix, g1_ref[...], gate_ref[...])


def _s5_layer(x, g0, sh, sc, gate, g1, packed, d_skip, w_glu):
    bsz, d = sh.shape
    n = x.size // d
    wb, wcr, wci, lbr, lbi = packed
    npack = wb.shape[0]
    rows = CHUNK * bsz
    if x.ndim == 3:
        x_spec = pl.BlockSpec((bsz, CHUNK, d), lambda t: (0, t, 0))
        relayout = [pltpu.VMEM((rows, d), F32)] + _relayout_scratch(rows, d)
    else:
        x_spec, relayout = _row_spec(rows, d), []
    return pl.pallas_call(
        _s5_kernel,
        grid=(n // rows,),
        in_specs=[x_spec, _const_spec((1, d)), _const_spec((bsz, d)),
                  _const_spec((bsz, d)), _const_spec((bsz, d)), _const_spec((1, d)),
                  _const_spec(wb.shape), _const_spec(lbr.shape), _const_spec(lbi.shape),
                  _const_spec(wcr.shape), _const_spec(wci.shape), _const_spec((1, d)),
                  _const_spec(w_glu.shape)],
        out_specs=_row_spec(rows, d),
        out_shape=jax.ShapeDtypeStruct((n, d), F32),
        scratch_shapes=[pltpu.VMEM((npack, bsz, 2 * S5_PACK_STATES), F32),
                        pltpu.VMEM((rows, d), F32),
                        [pltpu.VMEM((rows, 2 * S5_PACK_STATES), F32)] * 2,
                        pltpu.VMEM((rows, d), F32)] + relayout,
        compiler_params=_params("arbitrary"),
        name="s5_layer",
    )(x, g0, sh, sc, gate, g1, wb, lbr, lbi, wcr, wci, d_skip.reshape(1, d), w_glu)


def _rope_kernel(pos_ref, inv_ref, cos_ref, sin_ref):
    ang = pos_ref[...].astype(F32) * inv_ref[...]
    lane = lax.broadcasted_iota(jnp.int32, ang.shape, 1)
    first_half = lane % DA_HEAD_DIM < DA_HEAD_DIM // 2
    cos_ref[...] = jnp.cos(ang)
    sin_ref[...] = jnp.where(first_half, -1.0, 1.0) * jnp.sin(ang)


def _rope_tables(pos_rows):
    n = pos_rows.shape[0]
    half = DA_HEAD_DIM // 2
    inv = ROPE_THETA ** (-jnp.arange(half, dtype=F32) / half)
    inv = jnp.tile(inv, LANES // half).reshape(1, LANES)
    tm = min(n, ROW_TILE)
    return pl.pallas_call(
        _rope_kernel,
        grid=(n // tm,),
        in_specs=[_row_spec(tm, 1), pl.BlockSpec((1, LANES), lambda i: (0, 0))],
        out_specs=[_row_spec(tm, LANES)] * 2,
        out_shape=[jax.ShapeDtypeStruct((n, LANES), F32)] * 2,
        compiler_params=_params("parallel"),
        name="rope_tables",
    )(pos_rows, inv)


def _qkv_kernel(x_ref, g_ref, sh_ref, sc_ref, w_ref, cos_ref, sin_ref, q_ref, k_ref, v_ref,
                *tmp_refs):
    d = x_ref.shape[-1]
    h = _modulated_norm(x_ref[...], g_ref[...], sh_ref[...], sc_ref[...]).astype(BF16)
    cos, sin = cos_ref[...], sin_ref[...]
    lane = lax.broadcasted_iota(jnp.int32, cos.shape, 1)
    half = DA_HEAD_DIM // 2
    first_half = lane % DA_HEAD_DIM < half

    for src, dst, scale in ((0, q_ref, DA_HEAD_DIM ** -0.5), (d, k_ref, 1.0)):
        t = _dot(h, w_ref[:, src:src + d])
        for j in range(d // LANES):
            tj = t[:, j * LANES:(j + 1) * LANES]
            rot = jnp.where(first_half, pltpu.roll(tj, LANES - half, 1), pltpu.roll(tj, half, 1))
            _put_batch_major(dst, j, (tj * cos + rot * sin) * scale, tmp_refs[j])
    t = _dot(h, w_ref[:, 2 * d:])
    for j in range(d // LANES):
        _put_batch_major(v_ref, j, t[:, j * LANES:(j + 1) * LANES], tmp_refs[j])


def _qkv_proj(x, g, sh, sc, w_qkv, cos, sin):
    n, d = x.shape
    bsz = sh.shape[0]
    tm = min(n, ROW_TILE)
    batch_major = pl.BlockSpec((bsz, tm // bsz, d), lambda i: (0, i, 0))
    return pl.pallas_call(
        _qkv_kernel,
        grid=(n // tm,),
        in_specs=[_row_spec(tm, d), _const_spec((1, d)), _const_spec((bsz, d)),
                  _const_spec((bsz, d)), _const_spec(w_qkv.shape), _row_spec(tm, LANES),
                  _row_spec(tm, LANES)],
        out_specs=[batch_major] * 3,
        out_shape=[jax.ShapeDtypeStruct((bsz, n // bsz, d), BF16)] * 3,
        scratch_shapes=_relayout_scratch(tm, d),
        compiler_params=_params("parallel"),
        name="qkv_proj",
    )(x, g, sh, sc, w_qkv, cos, sin)


def _attn_kernel(q_ref, k_ref, v_ref, lq1_ref, lk1_ref, lq2_ref, lk2_ref, sg_ref, o_ref,
                 kt_ref, vext_ref, qs_ref, m_ref, acc_ref, s_refs, p_refs, a_refs, *,
                 lambda_init, tq):
    seq = q_ref.shape[0]
    vext_ref[:, :DA_V_DIM] = v_ref[...]
    col = lax.broadcasted_iota(jnp.int32, (seq, DA_V_DIM), 1)
    vext_ref[:, DA_V_DIM:] = jnp.where(col == 0, 1.0, 0.0).astype(BF16)
    tb = min(seq, ROW_TILE)
    for blk in range(seq // tb):
        kt_ref[:, blk * tb:(blk + 1) * tb] = (
            k_ref[blk * tb:(blk + 1) * tb, :].astype(F32).T.astype(BF16))

    lam = (jnp.exp(jnp.sum(lq1_ref[...] * lk1_ref[...], keepdims=True))
           - jnp.exp(jnp.sum(lq2_ref[...] * lk2_ref[...], keepdims=True)) + lambda_init)

    lane = lax.broadcasted_iota(jnp.int32, (tq, 2 * DA_HEAD_DIM), 1)
    first = lane < DA_HEAD_DIM
    rows = 2 * tq
    tk = tq // 2

    def scores(t, slot):
        kt = kt_ref[:, pl.ds(pl.multiple_of(t * tk, tk), tk)]
        s_refs[slot][...] = _dot(qs_ref[...], kt)

    def softmax(slot, diag):
        for i in range(rows // ATTN_SM_ROWS):
            rs = pl.ds(i * ATTN_SM_ROWS, ATTN_SM_ROWS)
            s = s_refs[slot][rs, :]
            if diag is not None:
                qrow = (i * ATTN_SM_ROWS) % tq + lax.broadcasted_iota(jnp.int32, s.shape, 0)
                key = diag * tk + lax.broadcasted_iota(jnp.int32, s.shape, 1)
                s = jnp.where(key < (qrow // CHUNK + 1) * CHUNK, s, NEG_BIG)
            parts = [s[:, j * LANES:(j + 1) * LANES] for j in range(tk // LANES)]
            m_prev = m_ref[rs, :]
            m_new = jnp.maximum(
                m_prev, jnp.max(functools.reduce(jnp.maximum, parts), axis=-1, keepdims=True))
            p_refs[slot][rs, :] = jnp.concatenate(
                [jnp.exp(pj - m_new) for pj in parts], axis=1).astype(BF16)
            a_refs[slot][rs, :] = jnp.exp(m_prev - m_new)
            m_ref[rs, :] = m_new

    def values(t, slot):
        vb = vext_ref[pl.ds(pl.multiple_of(t * tk, tk), tk), :]
        alpha = a_refs[slot][...]
        acc_ref[...] = (jnp.concatenate([alpha, alpha], axis=1) * acc_ref[...]
                        + _dot(p_refs[slot][...], vb))

    def q_block(qi, carry):
        q0 = pl.multiple_of(qi * tq, tq)
        qb = q_ref[pl.ds(q0, tq), :]
        zero = jnp.zeros_like(qb)
        qs_ref[:tq, :] = jnp.where(first, qb, zero)
        qs_ref[tq:, :] = jnp.where(first, zero, qb)
        acc_ref[...] = jnp.zeros_like(acc_ref)
        m_ref[...] = jnp.full(m_ref.shape, NEG_BIG, F32)
        p_refs[1][...] = jnp.zeros_like(p_refs[1])
        a_refs[1][...] = jnp.ones_like(a_refs[1])

        def pair(u, last):
            t = 2 * u
            scores(t + 1, 1)
            softmax(0, 0 if last else None)
            values(jnp.maximum(t - 1, 0), 1)
            if not last:
                scores(t + 2, 0)
            softmax(1, 1 if last else None)
            values(t, 0)

        def full_pair(u, carry):
            pair(u, False)
            return carry

        scores(0, 0)
        lax.fori_loop(0, qi, full_pair, 0)
        pair(qi, True)
        values(2 * qi + 1, 1)

        for i in range(tq // ATTN_SM_ROWS):
            o = []
            for comp in range(2):
                acc = acc_ref[pl.ds(comp * tq + i * ATTN_SM_ROWS, ATTN_SM_ROWS), :]
                o.append(acc[:, :DA_V_DIM] / acc[:, DA_V_DIM:DA_V_DIM + 1])
            od = _rms(o[0] - lam * o[1], sg_ref[...]) * (1.0 - lambda_init)
            o_ref[pl.ds(q0 + i * ATTN_SM_ROWS, ATTN_SM_ROWS), :] = od.astype(BF16)
        return carry

    lax.fori_loop(0, seq // tq, q_block, 0)


def _diff_attention(q, k, v, lq1, lk1, lq2, lk2, subln_g, lambda_init):
    bsz, seq, d = q.shape
    tq = min(seq, ATTN_TQ)
    rows, tk = 2 * tq, tq // 2
    head = pl.BlockSpec((None, seq, DA_V_DIM), lambda b, h: (b, 0, h))
    vec = _const_spec((1, DA_HEAD_DIM))
    return pl.pallas_call(
        functools.partial(_attn_kernel, lambda_init=lambda_init, tq=tq),
        grid=(bsz, d // DA_V_DIM),
        in_specs=[head, head, head, vec, vec, vec, vec, _const_spec((1, DA_V_DIM))],
        out_specs=head,
        out_shape=jax.ShapeDtypeStruct((bsz, seq, d), BF16),
        scratch_shapes=[pltpu.VMEM((2 * DA_HEAD_DIM, seq), BF16),
                        pltpu.VMEM((seq, 2 * DA_V_DIM), BF16),
                        pltpu.VMEM((rows, 2 * DA_HEAD_DIM), BF16),
                        pltpu.VMEM((rows, LANES), F32),
                        pltpu.VMEM((rows, 2 * DA_V_DIM), F32),
                        [pltpu.VMEM((rows, tk), F32)] * 2,
                        [pltpu.VMEM((rows, tk), BF16)] * 2,
                        [pltpu.VMEM((rows, LANES), F32)] * 2],
        compiler_params=_params("parallel", "parallel"),
        name="diff_attention",
    )(q, k, v, lq1.reshape(1, -1), lk1.reshape(1, -1), lq2.reshape(1, -1), lk2.reshape(1, -1),
      subln_g.reshape(1, -1))


def _oproj_kernel(o_ref, x_ref, w_ref, g_ref, gate_ref, out_ref, *tmp_refs):
    o = _get_time_major(o_ref, tmp_refs).astype(BF16)
    out_ref[...] = _gated_residual(x_ref[...], _dot(o, w_ref[...]), g_ref[...], gate_ref[...])


def _out_proj(o, x, w_o, g, gate):
    n, d = x.shape
    bsz = gate.shape[0]
    tm = min(n, ROW_TILE)
    return pl.pallas_call(
        _oproj_kernel,
        grid=(n // tm,),
        in_specs=[pl.BlockSpec((bsz, tm // bsz, d), lambda i: (0, i, 0)), _row_spec(tm, d),
                  _const_spec(w_o.shape), _const_spec((1, d)), _const_spec((bsz, d))],
        out_specs=_row_spec(tm, d),
        out_shape=jax.ShapeDtypeStruct(x.shape, F32),
        scratch_shapes=_relayout_scratch(tm, d),
        compiler_params=_params("parallel"),
        name="attn_out_proj",
    )(o, x, w_o, g, gate)


def _ffn_kernel(x_ref, xprev_ref, g2_ref, sh_ref, sc_ref, gate_ref, g3_ref, win_ref, cw_ref, cb_ref,
                wout_ref, out_ref, h_ref, acc_ref, u_refs, g_refs, *tmp_refs):
    tm, d = x_ref.shape
    f = wout_ref.shape[0]
    bsz = sh_ref.shape[0]
    halo = xprev_ref.shape[0]
    tf = FFN_TF
    n_tiles = f // tf
    g2, sh, sc = g2_ref[...], sh_ref[...], sc_ref[...]
    h_ref[:halo, :] = _modulated_norm(xprev_ref[...], g2, sh, sc).astype(BF16)
    h_ref[halo:, :] = _modulated_norm(x_ref[...], g2, sh, sc).astype(BF16)
    keep_halo = jnp.where(pl.program_id(0) > 0, 1.0, 0.0)
    acc_ref[...] = jnp.zeros_like(acc_ref)

    def up(i, slot):
        c0 = pl.multiple_of(i * tf, tf)
        hb = h_ref[...]
        for half, col in enumerate((pl.ds(c0, tf), pl.ds(f + c0, tf))):
            u_refs[slot][:, half * tf:(half + 1) * tf] = _dot(hb, win_ref[:, col])
        u_refs[slot][:halo, :] = u_refs[slot][:halo, :] * keep_halo

    def gate(i, slot):
        c0 = pl.multiple_of(i * tf, tf)
        halves = []
        for half, col in enumerate((pl.ds(c0, tf), pl.ds(f + c0, tf))):
            w = cw_ref[:, col]
            out = cb_ref[:, col]
            for j in range(CONV_WIDTH):
                out = out + w[j:j + 1] * u_refs[slot][pl.ds(j * bsz, tm), half * tf:(half + 1) * tf]
            halves.append(out)
        g_refs[slot][...] = (jax.nn.gelu(halves[0]) * halves[1]).astype(BF16)

    def down(i, slot):
        c0 = pl.multiple_of(i * tf, tf)
        acc_ref[...] += _dot(g_refs[slot][...], wout_ref[pl.ds(c0, tf), :])

    def pair(v, carry):
        i = 2 * v
        up(i + 1, 1)
        gate(i, 0)
        down(i, 0)
        up(i + 2, 0)
        gate(i + 1, 1)
        down(i + 1, 1)
        return carry

    assert n_tiles % 2 == 1
    up(0, 0)
    lax.fori_loop(0, n_tiles // 2, pair, 0)
    gate(n_tiles - 1, 0)
    down(n_tiles - 1, 0)
    res = _gated_residual(x_ref[...], acc_ref[...], g3_ref[...], gate_ref[...])
    if tmp_refs:
        for j, tmp_ref in enumerate(tmp_refs):
            _put_batch_major(out_ref, j, res[:, j * LANES:(j + 1) * LANES], tmp_ref)
    else:
        out_ref[...] = res


def _conv_ffn(x, g2, sh, sc, gate, g3, w_in, conv_w, conv_b, w_out, batch_major_out=False):
    n, d = x.shape
    f = w_out.shape[0]
    bsz = sh.shape[0]
    tm = min(n, ROW_TILE)
    halo = (CONV_WIDTH - 1) * bsz
    prev = pl.BlockSpec((halo, d), lambda i: (jnp.maximum(i * (tm // halo) - 1, 0), 0))
    vec = _const_spec((bsz, d))
    if batch_major_out:
        out_spec = pl.BlockSpec((bsz, tm // bsz, d), lambda i: (0, i, 0))
        out_shape, relayout = (bsz, n // bsz, d), _relayout_scratch(tm, d)
    else:
        out_spec, out_shape, relayout = _row_spec(tm, d), (n, d), []
    return pl.pallas_call(
        _ffn_kernel,
        grid=(n // tm,),
        in_specs=[_row_spec(tm, d), prev, _const_spec((1, d)), vec, vec, vec, _const_spec((1, d)),
                  _const_spec(w_in.shape), _const_spec(conv_w.shape), _const_spec((1, 2 * f)),
                  _const_spec(w_out.shape)],
        out_specs=out_spec,
        out_shape=jax.ShapeDtypeStruct(out_shape, F32),
        scratch_shapes=[pltpu.VMEM((tm + halo, d), BF16),
                        pltpu.VMEM((tm, d), F32),
                        [pltpu.VMEM((tm + halo, 2 * FFN_TF), F32)] * 2,
                        [pltpu.VMEM((tm, FFN_TF), BF16)] * 2] + relayout,
        compiler_params=_params("parallel"),
        name="conv_ffn",
    )(x, x, g2, sh, sc, gate, g3, w_in, conv_w, conv_b.reshape(1, 2 * f), w_out)


def kernel(x, c, positions, ada_w, ada_b, norm_g, s5_a_re, s5_a_im, s5_log_dt, s5_b_re, s5_b_im,
           s5_c_re, s5_c_im, s5_d, s5_w_glu, da_w_qkv, da_w_o, da_lq1, da_lk1, da_lq2, da_lk2,
           da_subln_g, ffn_w_in, ffn_conv_w, ffn_conv_b, ffn_w_out):
    depth = ada_w.shape[0]
    bsz, seq, d = x.shape
    n = seq * bsz
    mod = _ada_mod(c, ada_w, ada_b).reshape(depth, bsz, 6, d)
    lb_re, lb_im, bb_re, bb_im = _s5_discretise(s5_a_re, s5_a_im, s5_log_dt, s5_b_re, s5_b_im)
    cos, sin = _rope_tables(positions.T.reshape(n, 1))
    for i in range(depth):
        sh_t, sc_t, g_t, sh_c, sc_c, g_c = (mod[i, :, m] for m in range(6))
        gains = norm_g[i].reshape(4, 1, d)
        j = i // 2
        if i % 2 == 0:
            packed = _s5_pack_weights(lb_re[j], lb_im[j], bb_re[j], bb_im[j], s5_c_re[j], s5_c_im[j])
            x = _s5_layer(x, gains[0], sh_t, sc_t, g_t, gains[1], packed, s5_d[j],
                          s5_w_glu[j].astype(BF16))
        else:
            lambda_init = 0.8 - 0.6 * math.exp(-0.3 * i)
            q, k, v = _qkv_proj(x, gains[0], sh_t, sc_t, da_w_qkv[j].astype(BF16), cos, sin)
            o = _diff_attention(q, k, v, da_lq1[j], da_lk1[j], da_lq2[j], da_lk2[j],
                                da_subln_g[j], lambda_init)
            x = _out_proj(o, x, da_w_o[j].astype(BF16), gains[1], g_t)
        x = _conv_ffn(x, gains[2], sh_c, sc_c, g_c, gains[3], ffn_w_in[i].astype(BF16),
                      ffn_conv_w[i], ffn_conv_b[i], ffn_w_out[i].astype(BF16),
                      batch_major_out=(i == depth - 1))
    return x
```

```python
import functools
import math

import jax
import jax.numpy as jnp
from jax import lax
from jax.experimental import pallas as pl
from jax.experimental.pallas import tpu as pltpu

F32 = jnp.float32
BF16 = jnp.bfloat16

EPS = 1e-6
CHUNK = 64
S5_GROUP = 16
S5_STATE = 64
S5_LAMBDA_RE_MAX = -1e-4
DA_HEADS = 8
DA_HEAD_DIM = 64
DA_V_DIM = 2 * DA_HEAD_DIM
ROPE_THETA = 10000.0
CONV_WIDTH = 3

LANES = 128
MXU_DIM = 256
VMEM_LIMIT_BYTES = 56 * 1024 * 1024

S5_PACK_GROUPS = MXU_DIM // S5_GROUP
S5_PACK_STATES = S5_PACK_GROUPS * S5_STATE
S5_SCAN_LANES = 4 * LANES
S5_FRAMES = 128

ROW_TILE = 1024
ATTN_TQ = 512
ATTN_SM_ROWS = 128
FFN_TF = 256
NEG_BIG = -1e30


def _params(*sem):
    return pltpu.CompilerParams(dimension_semantics=sem, vmem_limit_bytes=VMEM_LIMIT_BYTES)


def _const_spec(shape):
    nd = len(shape)
    return pl.BlockSpec(shape, lambda *_: (0,) * nd, pipeline_mode=pl.Buffered(1))


def _row_spec(tm, width):
    return pl.BlockSpec((tm, width), lambda i: (i, 0))


def _rms(x, g):
    ms = jnp.mean(x * x, axis=-1, keepdims=True)
    return x * lax.rsqrt(ms + EPS) * g


def _per_seq(x, bsz, fn):
    rows, d = x.shape
    if bsz == 1:
        return fn(x)
    return fn(x.reshape(rows // bsz, bsz, d)).reshape(rows, d)


def _modulated_norm(x, g, sh, sc):
    return _per_seq(_rms(x, g), sh.shape[0], lambda y: y * (1.0 + sc) + sh)


def _gated_residual(x, y, g, gate):
    return x + _per_seq(_rms(y, g), gate.shape[0], lambda r: gate * r)


def _dot(a, b):
    return jnp.dot(a, b, preferred_element_type=F32)


def _relayout_scratch(rows, d):
    return [pltpu.VMEM((rows, LANES), F32)] * (d // LANES)


def _put_batch_major(dst_ref, j, val, tmp_ref):
    bsz, tt, _ = dst_ref.shape
    tmp_ref[...] = val
    for b in range(bsz):
        dst_ref[b, :, j * LANES:(j + 1) * LANES] = (
            tmp_ref[pl.ds(b, tt, stride=bsz), :].astype(dst_ref.dtype))


def _get_time_major(src_ref, tmp_refs):
    bsz, tt, _ = src_ref.shape
    for j, tmp_ref in enumerate(tmp_refs):
        for b in range(bsz):
            tmp_ref[pl.ds(b, tt, stride=bsz), :] = src_ref[b, :, j * LANES:(j + 1) * LANES].astype(F32)
    return jnp.concatenate([t[...] for t in tmp_refs], axis=1)


def _ada_kernel(c_ref, w_ref, b_ref, o_ref):
    c = c_ref[...]
    cond = (c * jax.nn.sigmoid(c)).astype(BF16)
    o_ref[...] = _dot(cond, w_ref[...].astype(BF16)) + b_ref[...]


def _ada_mod(c, ada_w, ada_b):
    depth, d, n = ada_w.shape
    bsz = c.shape[0]
    tn = n // 4
    return pl.pallas_call(
        _ada_kernel,
        grid=(depth, n // tn),
        in_specs=[
            pl.BlockSpec((bsz, d), lambda i, j: (0, 0)),
            pl.BlockSpec((None, d, tn), lambda i, j: (i, 0, j)),
            pl.BlockSpec((None, 1, tn), lambda i, j: (i, 0, j)),
        ],
        out_specs=pl.BlockSpec((None, bsz, tn), lambda i, j: (i, 0, j)),
        out_shape=jax.ShapeDtypeStruct((depth, bsz, n), F32),
        compiler_params=_params("parallel", "parallel"),
        name="ada_mod",
    )(c, ada_w, ada_b.reshape(depth, 1, n))


def _s5_disc_kernel(are_ref, aim_ref, ldt_ref, br_ref, bi_ref, lbr_ref, lbi_ref, bbr_ref, bbi_ref):
    lam_re = jnp.minimum(are_ref[...], S5_LAMBDA_RE_MAX)
    lam_im = aim_ref[...]
    dt = jnp.exp(ldt_ref[...])
    dre, dimg = lam_re * dt, lam_im * dt
    mag = jnp.exp(dre)
    lb_re, lb_im = mag * jnp.cos(dimg), mag * jnp.sin(dimg)
    den = lam_re * lam_re + lam_im * lam_im
    nr = lb_re - 1.0
    f_re = (nr * lam_re + lb_im * lam_im) / den
    f_im = (lb_im * lam_re - nr * lam_im) / den
    br, bi = br_ref[...], bi_ref[...]
    lbr_ref[...] = lb_re
    lbi_ref[...] = lb_im
    bbr_ref[...] = f_re * br - f_im * bi
    bbi_ref[...] = f_re * bi + f_im * br


def _s5_discretise(a_re, a_im, log_dt, b_re, b_im):
    n, g, p = a_re.shape
    cg = b_re.shape[-1]
    vec = pl.BlockSpec((None, g, 1, p), lambda i: (i, 0, 0, 0))
    mat = pl.BlockSpec((None, g, cg, p), lambda i: (i, 0, 0, 0))
    return pl.pallas_call(
        _s5_disc_kernel,
        grid=(n,),
        in_specs=[vec, vec, pl.BlockSpec((None, g, 1, 1), lambda i: (i, 0, 0, 0)), mat, mat],
        out_specs=[vec, vec, mat, mat],
        out_shape=[jax.ShapeDtypeStruct((n, g, 1, p), F32)] * 2
        + [jax.ShapeDtypeStruct((n, g, cg, p), F32)] * 2,
        compiler_params=_params("parallel"),
        name="s5_discretise",
    )(a_re.reshape(n, g, 1, p), a_im.reshape(n, g, 1, p), log_dt.reshape(n, g, 1, 1),
      jnp.swapaxes(b_re, -1, -2), jnp.swapaxes(b_im, -1, -2))


def _s5_pack_weights(lb_re, lb_im, bb_re, bb_im, c_re, c_im):
    g, cg, p = bb_re.shape
    gp = S5_PACK_GROUPS
    npack = g // gp
    eye = jnp.eye(gp, dtype=F32)

    def in_proj(bb):
        return jnp.einsum("kgcp,gh->kgchp", bb.reshape(npack, gp, cg, p), eye).reshape(
            npack, gp * cg, gp * p)

    def out_proj(c):
        return jnp.einsum("kgcp,gh->kgphc", c.reshape(npack, gp, cg, p), eye).reshape(
            npack, gp * p, gp * cg)

    wb = jnp.concatenate([in_proj(bb_re), in_proj(bb_im)], axis=-1).astype(BF16)
    return (wb, out_proj(c_re).astype(BF16), out_proj(c_im).astype(BF16),
            lb_re.reshape(npack, 1, gp * p), lb_im.reshape(npack, 1, gp * p))


def _s5_kernel(x_ref, g0_ref, sh_ref, sc_ref, gate_ref, g1_ref, wb_ref, lbr_ref, lbi_ref,
               wcr_ref, wci_ref, dsk_ref, wglu_ref, out_ref, h_ref, u_ref, bu_refs, y_ref,
               *relayout):
    if relayout:
        xtm_ref, *tmp_refs = relayout
        xtm_ref[...] = _get_time_major(x_ref, tmp_refs)
        x_ref = xtm_ref
    rows, d = x_ref.shape
    bsz = sh_ref.shape[0]
    npack = wb_ref.shape[0]
    ns = S5_PACK_STATES

    @pl.when(pl.program_id(0) == 0)
    def _():
        h_ref[...] = jnp.zeros_like(h_ref)

    u_ref[...] = _modulated_norm(x_ref[...], g0_ref[...], sh_ref[...], sc_ref[...])

    for k in range(npack):
        c0 = k * MXU_DIM
        bu_ref = bu_refs[k % 2]
        bu_ref[...] = _dot(u_ref[:, c0:c0 + MXU_DIM].astype(BF16), wb_ref[k])
        for s in range(ns // S5_SCAN_LANES):
            re = pl.ds(s * S5_SCAN_LANES, S5_SCAN_LANES)
            im = pl.ds(ns + s * S5_SCAN_LANES, S5_SCAN_LANES)
            lbr = jnp.broadcast_to(lbr_ref[k, :, re], (bsz, S5_SCAN_LANES))
            lbi = jnp.broadcast_to(lbi_ref[k, :, re], (bsz, S5_SCAN_LANES))
            hr, hi = h_ref[k, :, re], h_ref[k, :, im]
            for t in range(rows // bsz):
                frame = pl.ds(t * bsz, bsz)
                hr, hi = (lbr * hr - lbi * hi + bu_ref[frame, re],
                          lbr * hi + lbi * hr + bu_ref[frame, im])
                bu_ref[frame, re] = hr
                bu_ref[frame, im] = hi
            h_ref[k, :, re] = hr
            h_ref[k, :, im] = hi
        y_ref[:, c0:c0 + MXU_DIM] = (_dot(bu_ref[:, :ns].astype(BF16), wcr_ref[k])
                                     - _dot(bu_ref[:, ns:].astype(BF16), wci_ref[k]))

    y = y_ref[...] + dsk_ref[...] * u_ref[...]
    o = _dot(jax.nn.gelu(y).astype(BF16), wglu_ref[...])
    mix = o[:, :d] * jax.nn.sigmoid(o[:, d:])
    out_ref[...] = _gated_residual(x_ref[...], mix, g1_ref[...], gate_ref[...])


def _s5_layer(x, g0, sh, sc, gate, g1, packed, d_skip, w_glu):
    bsz, d = sh.shape
    n = x.size // d
    wb, wcr, wci, lbr, lbi = packed
    npack = wb.shape[0]
    rows = S5_FRAMES * bsz
    if x.ndim == 3:
        x_spec = pl.BlockSpec((bsz, S5_FRAMES, d), lambda t: (0, t, 0))
        relayout = [pltpu.VMEM((rows, d), F32)] + _relayout_scratch(rows, d)
    else:
        x_spec, relayout = _row_spec(rows, d), []
    return pl.pallas_call(
        _s5_kernel,
        grid=(n // rows,),
        in_specs=[x_spec, _const_spec((1, d)), _const_spec((bsz, d)),
                  _const_spec((bsz, d)), _const_spec((bsz, d)), _const_spec((1, d)),
                  _const_spec(wb.shape), _const_spec(lbr.shape), _const_spec(lbi.shape),
                  _const_spec(wcr.shape), _const_spec(wci.shape), _const_spec((1, d)),
                  _const_spec(w_glu.shape)],
        out_specs=_row_spec(rows, d),
        out_shape=jax.ShapeDtypeStruct((n, d), F32),
        scratch_shapes=[pltpu.VMEM((npack, bsz, 2 * S5_PACK_STATES), F32),
                        pltpu.VMEM((rows, d), F32),
                        [pltpu.VMEM((rows, 2 * S5_PACK_STATES), F32)] * 2,
                        pltpu.VMEM((rows, d), F32)] + relayout,
        compiler_params=_params("arbitrary"),
        name="s5_layer",
    )(x, g0, sh, sc, gate, g1, wb, lbr, lbi, wcr, wci, d_skip.reshape(1, d), w_glu)


def _rope_kernel(pos_ref, inv_ref, cos_ref, sin_ref):
    ang = pos_ref[...].astype(F32) * inv_ref[...]
    lane = lax.broadcasted_iota(jnp.int32, ang.shape, 1)
    first_half = lane % DA_HEAD_DIM < DA_HEAD_DIM // 2
    cos_ref[...] = jnp.cos(ang)
    sin_ref[...] = jnp.where(first_half, -1.0, 1.0) * jnp.sin(ang)


def _rope_tables(pos_rows):
    n = pos_rows.shape[0]
    half = DA_HEAD_DIM // 2
    inv = ROPE_THETA ** (-jnp.arange(half, dtype=F32) / half)
    inv = jnp.tile(inv, LANES // half).reshape(1, LANES)
    tm = min(n, ROW_TILE)
    return pl.pallas_call(
        _rope_kernel,
        grid=(n // tm,),
        in_specs=[_row_spec(tm, 1), pl.BlockSpec((1, LANES), lambda i: (0, 0))],
        out_specs=[_row_spec(tm, LANES)] * 2,
        out_shape=[jax.ShapeDtypeStruct((n, LANES), F32)] * 2,
        compiler_params=_params("parallel"),
        name="rope_tables",
    )(pos_rows, inv)


def _qkv_kernel(x_ref, g_ref, sh_ref, sc_ref, w_ref, cos_ref, sin_ref, q_ref, k_ref, v_ref,
                *tmp_refs):
    d = x_ref.shape[-1]
    h = _modulated_norm(x_ref[...], g_ref[...], sh_ref[...], sc_ref[...]).astype(BF16)
    cos, sin = cos_ref[...], sin_ref[...]
    lane = lax.broadcasted_iota(jnp.int32, cos.shape, 1)
    half = DA_HEAD_DIM // 2
    first_half = lane % DA_HEAD_DIM < half

    for src, dst, scale in ((0, q_ref, DA_HEAD_DIM ** -0.5), (d, k_ref, 1.0)):
        t = _dot(h, w_ref[:, src:src + d])
        for j in range(d // LANES):
            tj = t[:, j * LANES:(j + 1) * LANES]
            rot = jnp.where(first_half, pltpu.roll(tj, LANES - half, 1), pltpu.roll(tj, half, 1))
            _put_batch_major(dst, j, (tj * cos + rot * sin) * scale, tmp_refs[j])
    t = _dot(h, w_ref[:, 2 * d:])
    for j in range(d // LANES):
        _put_batch_major(v_ref, j, t[:, j * LANES:(j + 1) * LANES], tmp_refs[j])


def _qkv_proj(x, g, sh, sc, w_qkv, cos, sin):
    n, d = x.shape
    bsz = sh.shape[0]
    tm = min(n, ROW_TILE)
    batch_major = pl.BlockSpec((bsz, tm // bsz, d), lambda i: (0, i, 0))
    return pl.pallas_call(
        _qkv_kernel,
        grid=(n // tm,),
        in_specs=[_row_spec(tm, d), _const_spec((1, d)), _const_spec((bsz, d)),
                  _const_spec((bsz, d)), _const_spec(w_qkv.shape), _row_spec(tm, LANES),
                  _row_spec(tm, LANES)],
        out_specs=[batch_major] * 3,
        out_shape=[jax.ShapeDtypeStruct((bsz, n // bsz, d), BF16)] * 3,
        scratch_shapes=_relayout_scratch(tm, d),
        compiler_params=_params("parallel"),
        name="qkv_proj",
    )(x, g, sh, sc, w_qkv, cos, sin)


def _attn_kernel(q_ref, k_ref, v_ref, lq1_ref, lk1_ref, lq2_ref, lk2_ref, sg_ref, o_ref,
                 kt_ref, vext_ref, qs_ref, m_ref, acc_ref, s_refs, p_refs, a_refs, *,
                 lambda_init, tq):
    seq = q_ref.shape[0]
    vext_ref[:, :DA_V_DIM] = v_ref[...]
    col = lax.broadcasted_iota(jnp.int32, (seq, DA_V_DIM), 1)
    vext_ref[:, DA_V_DIM:] = jnp.where(col == 0, 1.0, 0.0).astype(BF16)
    tb = min(seq, ROW_TILE)
    for blk in range(seq // tb):
        kt_ref[:, blk * tb:(blk + 1) * tb] = (
            k_ref[blk * tb:(blk + 1) * tb, :].astype(F32).T.astype(BF16))

    lam = (jnp.exp(jnp.sum(lq1_ref[...] * lk1_ref[...], keepdims=True))
           - jnp.exp(jnp.sum(lq2_ref[...] * lk2_ref[...], keepdims=True)) + lambda_init)

    lane = lax.broadcasted_iota(jnp.int32, (tq, 2 * DA_HEAD_DIM), 1)
    first = lane < DA_HEAD_DIM
    rows = 2 * tq
    tk = tq // 2

    def row_ranges(diag):
        if not diag:
            return [(0, rows)]
        return [(comp * tq + diag * tk, (comp + 1) * tq) for comp in range(2)]

    def scores(t, slot, diag=None):
        kt = kt_ref[:, pl.ds(pl.multiple_of(t * tk, tk), tk)]
        for r0, r1 in row_ranges(diag):
            s_refs[slot][r0:r1, :] = _dot(qs_ref[r0:r1, :], kt)

    def softmax(slot, diag):
        for r0, r1 in row_ranges(diag):
            for i0 in range(r0, r1, ATTN_SM_ROWS):
                rs = pl.ds(i0, ATTN_SM_ROWS)
                s = s_refs[slot][rs, :]
                if diag is not None:
                    qrow = i0 % tq + lax.broadcasted_iota(jnp.int32, s.shape, 0)
                    key = diag * tk + lax.broadcasted_iota(jnp.int32, s.shape, 1)
                    s = jnp.where(key < (qrow // CHUNK + 1) * CHUNK, s, NEG_BIG)
                parts = [s[:, j * LANES:(j + 1) * LANES] for j in range(tk // LANES)]
                m_prev = m_ref[rs, :]
                m_new = jnp.maximum(
                    m_prev, jnp.max(functools.reduce(jnp.maximum, parts), axis=-1, keepdims=True))
                p_refs[slot][rs, :] = jnp.concatenate(
                    [jnp.exp(pj - m_new) for pj in parts], axis=1).astype(BF16)
                a_refs[slot][rs, :] = jnp.exp(m_prev - m_new)
                m_ref[rs, :] = m_new

    def values(t, slot, diag=None):
        vb = vext_ref[pl.ds(pl.multiple_of(t * tk, tk), tk), :]
        for r0, r1 in row_ranges(diag):
            alpha = a_refs[slot][r0:r1, :]
            acc_ref[r0:r1, :] = (jnp.concatenate([alpha, alpha], axis=1) * acc_ref[r0:r1, :]
                                 + _dot(p_refs[slot][r0:r1, :], vb))

    def q_block(qi, carry):
        q0 = pl.multiple_of(qi * tq, tq)
        qb = q_ref[pl.ds(q0, tq), :]
        zero = jnp.zeros_like(qb)
        qs_ref[:tq, :] = jnp.where(first, qb, zero)
        qs_ref[tq:, :] = jnp.where(first, zero, qb)
        acc_ref[...] = jnp.zeros_like(acc_ref)
        m_ref[...] = jnp.full(m_ref.shape, NEG_BIG, F32)
        p_refs[1][...] = jnp.zeros_like(p_refs[1])
        a_refs[1][...] = jnp.ones_like(a_refs[1])

        def pair(u, last):
            t = 2 * u
            scores(t + 1, 1, 1 if last else None)
            softmax(0, 0 if last else None)
            values(jnp.maximum(t - 1, 0), 1)
            if not last:
                scores(t + 2, 0)
            softmax(1, 1 if last else None)
            values(t, 0)

        def full_pair(u, carry):
            pair(u, False)
            return carry

        scores(0, 0)
        lax.fori_loop(0, qi, full_pair, 0)
        pair(qi, True)
        values(2 * qi + 1, 1, 1)

        for i in range(tq // ATTN_SM_ROWS):
            o = []
            for comp in range(2):
                acc = acc_ref[pl.ds(comp * tq + i * ATTN_SM_ROWS, ATTN_SM_ROWS), :]
                o.append(acc[:, :DA_V_DIM] / acc[:, DA_V_DIM:DA_V_DIM + 1])
            od = _rms(o[0] - lam * o[1], sg_ref[...]) * (1.0 - lambda_init)
            o_ref[pl.ds(q0 + i * ATTN_SM_ROWS, ATTN_SM_ROWS), :] = od.astype(BF16)
        return carry

    lax.fori_loop(0, seq // tq, q_block, 0)


def _diff_attention(q, k, v, lq1, lk1, lq2, lk2, subln_g, lambda_init):
    bsz, seq, d = q.shape
    tq = min(seq, ATTN_TQ)
    rows, tk = 2 * tq, tq // 2
    head = pl.BlockSpec((None, seq, DA_V_DIM), lambda b, h: (b, 0, h))
    vec = _const_spec((1, DA_HEAD_DIM))
    return pl.pallas_call(
        functools.partial(_attn_kernel, lambda_init=lambda_init, tq=tq),
        grid=(bsz, d // DA_V_DIM),
        in_specs=[head, head, head, vec, vec, vec, vec, _const_spec((1, DA_V_DIM))],
        out_specs=head,
        out_shape=jax.ShapeDtypeStruct((bsz, seq, d), BF16),
        scratch_shapes=[pltpu.VMEM((2 * DA_HEAD_DIM, seq), BF16),
                        pltpu.VMEM((seq, 2 * DA_V_DIM), BF16),
                        pltpu.VMEM((rows, 2 * DA_HEAD_DIM), BF16),
                        pltpu.VMEM((rows, LANES), F32),
                        pltpu.VMEM((rows, 2 * DA_V_DIM), F32),
                        [pltpu.VMEM((rows, tk), F32)] * 2,
                        [pltpu.VMEM((rows, tk), BF16)] * 2,
                        [pltpu.VMEM((rows, LANES), F32)] * 2],
        compiler_params=_params("parallel", "parallel"),
        name="diff_attention",
    )(q, k, v, lq1.reshape(1, -1), lk1.reshape(1, -1), lq2.reshape(1, -1), lk2.reshape(1, -1),
      subln_g.reshape(1, -1))


def _oproj_kernel(o_ref, x_ref, w_ref, g_ref, gate_ref, out_ref, *tmp_refs):
    o = _get_time_major(o_ref, tmp_refs).astype(BF16)
    out_ref[...] = _gated_residual(x_ref[...], _dot(o, w_ref[...]), g_ref[...], gate_ref[...])


def _out_proj(o, x, w_o, g, gate):
    n, d = x.shape
    bsz = gate.shape[0]
    tm = min(n, ROW_TILE)
    return pl.pallas_call(
        _oproj_kernel,
        grid=(n // tm,),
        in_specs=[pl.BlockSpec((bsz, tm // bsz, d), lambda i: (0, i, 0)), _row_spec(tm, d),
                  _const_spec(w_o.shape), _const_spec((1, d)), _const_spec((bsz, d))],
        out_specs=_row_spec(tm, d),
        out_shape=jax.ShapeDtypeStruct(x.shape, F32),
        scratch_shapes=_relayout_scratch(tm, d),
        compiler_params=_params("parallel"),
        name="attn_out_proj",
    )(o, x, w_o, g, gate)


def _ffn_kernel(x_ref, xprev_ref, g2_ref, sh_ref, sc_ref, gate_ref, g3_ref, win_ref, cw_ref, cb_ref,
                wout_ref, out_ref, h_ref, acc_ref, u_refs, g_refs, *tmp_refs):
    tm, d = x_ref.shape
    f = wout_ref.shape[0]
    bsz = sh_ref.shape[0]
    halo = xprev_ref.shape[0]
    tf = FFN_TF
    n_tiles = f // tf
    g2, sh, sc = g2_ref[...], sh_ref[...], sc_ref[...]
    h_ref[:halo, :] = _modulated_norm(xprev_ref[...], g2, sh, sc).astype(BF16)
    h_ref[halo:, :] = _modulated_norm(x_ref[...], g2, sh, sc).astype(BF16)
    keep_halo = jnp.where(pl.program_id(0) > 0, 1.0, 0.0)
    acc_ref[...] = jnp.zeros_like(acc_ref)

    def up(i, slot):
        c0 = pl.multiple_of(i * tf, tf)
        hb = h_ref[...]
        for half, col in enumerate((pl.ds(c0, tf), pl.ds(f + c0, tf))):
            u_refs[slot][:, half * tf:(half + 1) * tf] = _dot(hb, win_ref[:, col])
        u_refs[slot][:halo, :] = u_refs[slot][:halo, :] * keep_halo

    def gate(i, slot):
        c0 = pl.multiple_of(i * tf, tf)
        halves = []
        for half, col in enumerate((pl.ds(c0, tf), pl.ds(f + c0, tf))):
            w = cw_ref[:, col].astype(BF16)
            out = cb_ref[:, col].astype(BF16)
            for j in range(CONV_WIDTH):
                u = u_refs[slot][pl.ds(j * bsz, tm), half * tf:(half + 1) * tf]
                out = out + w[j:j + 1] * u.astype(BF16)
            halves.append(out)
        g_refs[slot][...] = jax.nn.gelu(halves[0]) * halves[1]

    def down(i, slot):
        c0 = pl.multiple_of(i * tf, tf)
        acc_ref[...] += _dot(g_refs[slot][...], wout_ref[pl.ds(c0, tf), :])

    def pair(v, carry):
        i = 2 * v
        up(i + 1, 1)
        gate(i, 0)
        down(i, 0)
        up(i + 2, 0)
        gate(i + 1, 1)
        down(i + 1, 1)
        return carry

    assert n_tiles % 2 == 1
    up(0, 0)
    lax.fori_loop(0, n_tiles // 2, pair, 0)
    gate(n_tiles - 1, 0)
    down(n_tiles - 1, 0)
    res = _gated_residual(x_ref[...], acc_ref[...], g3_ref[...], gate_ref[...])
    if tmp_refs:
        for j, tmp_ref in enumerate(tmp_refs):
            _put_batch_major(out_ref, j, res[:, j * LANES:(j + 1) * LANES], tmp_ref)
    else:
        out_ref[...] = res


def _conv_ffn(x, g2, sh, sc, gate, g3, w_in, conv_w, conv_b, w_out, batch_major_out=False):
    n, d = x.shape
    f = w_out.shape[0]
    bsz = sh.shape[0]
    tm = min(n, ROW_TILE)
    halo = (CONV_WIDTH - 1) * bsz
    prev = pl.BlockSpec((halo, d), lambda i: (jnp.maximum(i * (tm // halo) - 1, 0), 0))
    vec = _const_spec((bsz, d))
    if batch_major_out:
        out_spec = pl.BlockSpec((bsz, tm // bsz, d), lambda i: (0, i, 0))
        out_shape, relayout = (bsz, n // bsz, d), _relayout_scratch(tm, d)
    else:
        out_spec, out_shape, relayout = _row_spec(tm, d), (n, d), []
    return pl.pallas_call(
        _ffn_kernel,
        grid=(n // tm,),
        in_specs=[_row_spec(tm, d), prev, _const_spec((1, d)), vec, vec, vec, _const_spec((1, d)),
                  _const_spec(w_in.shape), _const_spec(conv_w.shape), _const_spec((1, 2 * f)),
                  _const_spec(w_out.shape)],
        out_specs=out_spec,
        out_shape=jax.ShapeDtypeStruct(out_shape, F32),
        scratch_shapes=[pltpu.VMEM((tm + halo, d), BF16),
                        pltpu.VMEM((tm, d), F32),
                        [pltpu.VMEM((tm + halo, 2 * FFN_TF), F32)] * 2,
                        [pltpu.VMEM((tm, FFN_TF), BF16)] * 2] + relayout,
        compiler_params=_params("parallel"),
        name="conv_ffn",
    )(x, x, g2, sh, sc, gate, g3, w_in, conv_w, conv_b.reshape(1, 2 * f), w_out)


def kernel(x, c, positions, ada_w, ada_b, norm_g, s5_a_re, s5_a_im, s5_log_dt, s5_b_re, s5_b_im,
           s5_c_re, s5_c_im, s5_d, s5_w_glu, da_w_qkv, da_w_o, da_lq1, da_lk1, da_lq2, da_lk2,
           da_subln_g, ffn_w_in, ffn_conv_w, ffn_conv_b, ffn_w_out):
    depth = ada_w.shape[0]
    bsz, seq, d = x.shape
    n = seq * bsz
    mod = _ada_mod(c, ada_w, ada_b).reshape(depth, bsz, 6, d)
    lb_re, lb_im, bb_re, bb_im = _s5_discretise(s5_a_re, s5_a_im, s5_log_dt, s5_b_re, s5_b_im)
    cos, sin = _rope_tables(positions.T.reshape(n, 1))
    for i in range(depth):
        sh_t, sc_t, g_t, sh_c, sc_c, g_c = (mod[i, :, m] for m in range(6))
        gains = norm_g[i].reshape(4, 1, d)
        j = i // 2
        if i % 2 == 0:
            packed = _s5_pack_weights(lb_re[j], lb_im[j], bb_re[j], bb_im[j], s5_c_re[j], s5_c_im[j])
            x = _s5_layer(x, gains[0], sh_t, sc_t, g_t, gains[1], packed, s5_d[j],
                          s5_w_glu[j].astype(BF16))
        else:
            lambda_init = 0.8 - 0.6 * math.exp(-0.3 * i)
            q, k, v = _qkv_proj(x, gains[0], sh_t, sc_t, da_w_qkv[j].astype(BF16), cos, sin)
            o = _diff_attention(q, k, v, da_lq1[j], da_lk1[j], da_lq2[j], da_lk2[j],
                                da_subln_g[j], lambda_init)
            x = _out_proj(o, x, da_w_o[j].astype(BF16), gains[1], g_t)
        x = _conv_ffn(x, gains[2], sh_c, sc_c, g_c, gains[3], ffn_w_in[i].astype(BF16),
                      ffn_conv_w[i], ffn_conv_b[i], ffn_w_out[i].astype(BF16),
                      batch_major_out=(i == depth - 1))
    return x
```

```python
import functools
import math

import jax
import jax.numpy as jnp
from jax import lax
from jax.experimental import pallas as pl
from jax.experimental.pallas import tpu as pltpu

F32 = jnp.float32
BF16 = jnp.bfloat16

EPS = 1e-6
CHUNK = 64
S5_GROUP = 16
S5_STATE = 64
S5_LAMBDA_RE_MAX = -1e-4
DA_HEADS = 8
DA_HEAD_DIM = 64
DA_V_DIM = 2 * DA_HEAD_DIM
ROPE_THETA = 10000.0
CONV_WIDTH = 3

LANES = 128
MXU_DIM = 256
VMEM_LIMIT_BYTES = 56 * 1024 * 1024

S5_PACK_GROUPS = MXU_DIM // S5_GROUP
S5_PACK_STATES = S5_PACK_GROUPS * S5_STATE
S5_SCAN_LANES = 4 * LANES
S5_FRAMES = 128

ROW_TILE = 1024
ATTN_TQ = 512
ATTN_SM_ROWS = 128
FFN_TF = 256
NEG_BIG = -1e30


def _params(*sem):
    return pltpu.CompilerParams(dimension_semantics=sem, vmem_limit_bytes=VMEM_LIMIT_BYTES)


def _const_spec(shape):
    nd = len(shape)
    return pl.BlockSpec(shape, lambda *_: (0,) * nd, pipeline_mode=pl.Buffered(1))


def _row_spec(tm, width):
    return pl.BlockSpec((tm, width), lambda i: (i, 0))


def _rms(x, g):
    ms = jnp.mean(x * x, axis=-1, keepdims=True)
    return x * lax.rsqrt(ms + EPS) * g


def _per_seq(x, bsz, fn):
    rows, d = x.shape
    if bsz == 1:
        return fn(x)
    return fn(x.reshape(rows // bsz, bsz, d)).reshape(rows, d)


def _modulated_norm(x, g, sh, sc):
    return _per_seq(_rms(x, g), sh.shape[0], lambda y: y * (1.0 + sc) + sh)


def _gated_residual(x, y, g, gate):
    return x + _per_seq(_rms(y, g), gate.shape[0], lambda r: gate * r)


def _dot(a, b):
    return jnp.dot(a, b, preferred_element_type=F32)


def _relayout_scratch(rows, d):
    return [pltpu.VMEM((rows, LANES), F32)] * (d // LANES)


def _put_batch_major(dst_ref, j, val, tmp_ref):
    bsz, tt, _ = dst_ref.shape
    tmp_ref[...] = val
    for b in range(bsz):
        dst_ref[b, :, j * LANES:(j + 1) * LANES] = (
            tmp_ref[pl.ds(b, tt, stride=bsz), :].astype(dst_ref.dtype))


def _get_time_major(src_ref, tmp_refs):
    bsz, tt, _ = src_ref.shape
    for j, tmp_ref in enumerate(tmp_refs):
        for b in range(bsz):
            tmp_ref[pl.ds(b, tt, stride=bsz), :] = src_ref[b, :, j * LANES:(j + 1) * LANES].astype(F32)
    return jnp.concatenate([t[...] for t in tmp_refs], axis=1)


def _ada_kernel(c_ref, w_ref, b_ref, o_ref):
    c = c_ref[...]
    cond = (c * jax.nn.sigmoid(c)).astype(BF16)
    o_ref[...] = _dot(cond, w_ref[...].astype(BF16)) + b_ref[...]


def _ada_mod(c, ada_w, ada_b):
    depth, d, n = ada_w.shape
    bsz = c.shape[0]
    tn = n // 4
    return pl.pallas_call(
        _ada_kernel,
        grid=(depth, n // tn),
        in_specs=[
            pl.BlockSpec((bsz, d), lambda i, j: (0, 0)),
            pl.BlockSpec((None, d, tn), lambda i, j: (i, 0, j)),
            pl.BlockSpec((None, 1, tn), lambda i, j: (i, 0, j)),
        ],
        out_specs=pl.BlockSpec((None, bsz, tn), lambda i, j: (i, 0, j)),
        out_shape=jax.ShapeDtypeStruct((depth, bsz, n), F32),
        compiler_params=_params("parallel", "parallel"),
        name="ada_mod",
    )(c, ada_w, ada_b.reshape(depth, 1, n))


def _s5_disc_kernel(are_ref, aim_ref, ldt_ref, br_ref, bi_ref, lbr_ref, lbi_ref, bbr_ref, bbi_ref):
    lam_re = jnp.minimum(are_ref[...], S5_LAMBDA_RE_MAX)
    lam_im = aim_ref[...]
    dt = jnp.exp(ldt_ref[...])
    dre, dimg = lam_re * dt, lam_im * dt
    mag = jnp.exp(dre)
    lb_re, lb_im = mag * jnp.cos(dimg), mag * jnp.sin(dimg)
    den = lam_re * lam_re + lam_im * lam_im
    nr = lb_re - 1.0
    f_re = (nr * lam_re + lb_im * lam_im) / den
    f_im = (lb_im * lam_re - nr * lam_im) / den
    br, bi = br_ref[...], bi_ref[...]
    lbr_ref[...] = lb_re
    lbi_ref[...] = lb_im
    bbr_ref[...] = f_re * br - f_im * bi
    bbi_ref[...] = f_re * bi + f_im * br


def _s5_discretise(a_re, a_im, log_dt, b_re, b_im):
    n, g, p = a_re.shape
    cg = b_re.shape[-1]
    vec = pl.BlockSpec((None, g, 1, p), lambda i: (i, 0, 0, 0))
    mat = pl.BlockSpec((None, g, cg, p), lambda i: (i, 0, 0, 0))
    return pl.pallas_call(
        _s5_disc_kernel,
        grid=(n,),
        in_specs=[vec, vec, pl.BlockSpec((None, g, 1, 1), lambda i: (i, 0, 0, 0)), mat, mat],
        out_specs=[vec, vec, mat, mat],
        out_shape=[jax.ShapeDtypeStruct((n, g, 1, p), F32)] * 2
        + [jax.ShapeDtypeStruct((n, g, cg, p), F32)] * 2,
        compiler_params=_params("parallel"),
        name="s5_discretise",
    )(a_re.reshape(n, g, 1, p), a_im.reshape(n, g, 1, p), log_dt.reshape(n, g, 1, 1),
      jnp.swapaxes(b_re, -1, -2), jnp.swapaxes(b_im, -1, -2))


def _s5_pack_weights(lb_re, lb_im, bb_re, bb_im, c_re, c_im):
    g, cg, p = bb_re.shape
    gp = S5_PACK_GROUPS
    npack = g // gp
    eye = jnp.eye(gp, dtype=F32)

    def in_proj(bb):
        return jnp.einsum("kgcp,gh->kgchp", bb.reshape(npack, gp, cg, p), eye).reshape(
            npack, gp * cg, gp * p)

    def out_proj(c):
        return jnp.einsum("kgcp,gh->kgphc", c.reshape(npack, gp, cg, p), eye).reshape(
            npack, gp * p, gp * cg)

    wb = jnp.concatenate([in_proj(bb_re), in_proj(bb_im)], axis=-1).astype(BF16)
    return (wb, out_proj(c_re).astype(BF16), out_proj(c_im).astype(BF16),
            lb_re.reshape(npack, 1, gp * p), lb_im.reshape(npack, 1, gp * p))


def _s5_kernel(x_ref, g0_ref, sh_ref, sc_ref, gate_ref, g1_ref, wb_ref, lbr_ref, lbi_ref,
               wcr_ref, wci_ref, dsk_ref, wglu_ref, out_ref, h_ref, u_ref, bu_refs, y_ref,
               *relayout):
    if relayout:
        xtm_ref, *tmp_refs = relayout
        xtm_ref[...] = _get_time_major(x_ref, tmp_refs)
        x_ref = xtm_ref
    rows, d = x_ref.shape
    bsz = sh_ref.shape[0]
    npack = wb_ref.shape[0]
    ns = S5_PACK_STATES

    @pl.when(pl.program_id(0) == 0)
    def _():
        h_ref[...] = jnp.zeros_like(h_ref)

    u_ref[...] = _modulated_norm(x_ref[...], g0_ref[...], sh_ref[...], sc_ref[...])

    for k in range(npack):
        c0 = k * MXU_DIM
        bu_ref = bu_refs[k % 2]
        bu_ref[...] = _dot(u_ref[:, c0:c0 + MXU_DIM].astype(BF16), wb_ref[k])
        for s in range(ns // S5_SCAN_LANES):
            re = pl.ds(s * S5_SCAN_LANES, S5_SCAN_LANES)
            im = pl.ds(ns + s * S5_SCAN_LANES, S5_SCAN_LANES)
            lbr = jnp.broadcast_to(lbr_ref[k, :, re], (bsz, S5_SCAN_LANES))
            lbi = jnp.broadcast_to(lbi_ref[k, :, re], (bsz, S5_SCAN_LANES))
            hr, hi = h_ref[k, :, re], h_ref[k, :, im]
            for t in range(rows // bsz):
                frame = pl.ds(t * bsz, bsz)
                hr, hi = (lbr * hr - lbi * hi + bu_ref[frame, re],
                          lbr * hi + lbi * hr + bu_ref[frame, im])
                bu_ref[frame, re] = hr
                bu_ref[frame, im] = hi
            h_ref[k, :, re] = hr
            h_ref[k, :, im] = hi
        y_ref[:, c0:c0 + MXU_DIM] = (_dot(bu_ref[:, :ns].astype(BF16), wcr_ref[k])
                                     - _dot(bu_ref[:, ns:].astype(BF16), wci_ref[k]))

    y = y_ref[...] + dsk_ref[...] * u_ref[...]
    o = _dot(jax.nn.gelu(y).astype(BF16), wglu_ref[...])
    mix = o[:, :d] * jax.nn.sigmoid(o[:, d:])
    out_ref[...] = _gated_residual(x_ref[...], mix, g1_ref[...], gate_ref[...])


def _s5_layer(x, g0, sh, sc, gate, g1, packed, d_skip, w_glu):
    bsz, d = sh.shape
    n = x.size // d
    wb, wcr, wci, lbr, lbi = packed
    npack = wb.shape[0]
    rows = S5_FRAMES * bsz
    if x.ndim == 3:
        x_spec = pl.BlockSpec((bsz, S5_FRAMES, d), lambda t: (0, t, 0))
        relayout = [pltpu.VMEM((rows, d), F32)] + _relayout_scratch(rows, d)
    else:
        x_spec, relayout = _row_spec(rows, d), []
    return pl.pallas_call(
        _s5_kernel,
        grid=(n // rows,),
        in_specs=[x_spec, _const_spec((1, d)), _const_spec((bsz, d)),
                  _const_spec((bsz, d)), _const_spec((bsz, d)), _const_spec((1, d)),
                  _const_spec(wb.shape), _const_spec(lbr.shape), _const_spec(lbi.shape),
                  _const_spec(wcr.shape), _const_spec(wci.shape), _const_spec((1, d)),
                  _const_spec(w_glu.shape)],
        out_specs=_row_spec(rows, d),
        out_shape=jax.ShapeDtypeStruct((n, d), F32),
        scratch_shapes=[pltpu.VMEM((npack, bsz, 2 * S5_PACK_STATES), F32),
                        pltpu.VMEM((rows, d), F32),
                        [pltpu.VMEM((rows, 2 * S5_PACK_STATES), F32)] * 2,
                        pltpu.VMEM((rows, d), F32)] + relayout,
        compiler_params=_params("arbitrary"),
        name="s5_layer",
    )(x, g0, sh, sc, gate, g1, wb, lbr, lbi, wcr, wci, d_skip.reshape(1, d), w_glu)


def _rope_kernel(pos_ref, inv_ref, cos_ref, sin_ref):
    ang = pos_ref[...].astype(F32) * inv_ref[...]
    lane = lax.broadcasted_iota(jnp.int32, ang.shape, 1)
    first_half = lane % DA_HEAD_DIM < DA_HEAD_DIM // 2
    cos_ref[...] = jnp.cos(ang)
    sin_ref[...] = jnp.where(first_half, -1.0, 1.0) * jnp.sin(ang)


def _rope_tables(pos_rows):
    n = pos_rows.shape[0]
    half = DA_HEAD_DIM // 2
    inv = ROPE_THETA ** (-jnp.arange(half, dtype=F32) / half)
    inv = jnp.tile(inv, LANES // half).reshape(1, LANES)
    tm = min(n, ROW_TILE)
    return pl.pallas_call(
        _rope_kernel,
        grid=(n // tm,),
        in_specs=[_row_spec(tm, 1), pl.BlockSpec((1, LANES), lambda i: (0, 0))],
        out_specs=[_row_spec(tm, LANES)] * 2,
        out_shape=[jax.ShapeDtypeStruct((n, LANES), F32)] * 2,
        compiler_params=_params("parallel"),
        name="rope_tables",
    )(pos_rows, inv)


def _qkv_kernel(x_ref, g_ref, sh_ref, sc_ref, w_ref, cos_ref, sin_ref, q_ref, k_ref, v_ref,
                *tmp_refs):
    d = x_ref.shape[-1]
    h = _modulated_norm(x_ref[...], g_ref[...], sh_ref[...], sc_ref[...]).astype(BF16)
    cos, sin = cos_ref[...], sin_ref[...]
    lane = lax.broadcasted_iota(jnp.int32, cos.shape, 1)
    half = DA_HEAD_DIM // 2
    first_half = lane % DA_HEAD_DIM < half

    for src, dst, scale in ((0, q_ref, DA_HEAD_DIM ** -0.5), (d, k_ref, 1.0)):
        t = _dot(h, w_ref[:, src:src + d])
        for j in range(d // LANES):
            tj = t[:, j * LANES:(j + 1) * LANES]
            rot = jnp.where(first_half, pltpu.roll(tj, LANES - half, 1), pltpu.roll(tj, half, 1))
            _put_batch_major(dst, j, (tj * cos + rot * sin) * scale, tmp_refs[j])
    t = _dot(h, w_ref[:, 2 * d:])
    for j in range(d // LANES):
        _put_batch_major(v_ref, j, t[:, j * LANES:(j + 1) * LANES], tmp_refs[j])


def _qkv_proj(x, g, sh, sc, w_qkv, cos, sin):
    n, d = x.shape
    bsz = sh.shape[0]
    tm = min(n, ROW_TILE)
    batch_major = pl.BlockSpec((bsz, tm // bsz, d), lambda i: (0, i, 0))
    return pl.pallas_call(
        _qkv_kernel,
        grid=(n // tm,),
        in_specs=[_row_spec(tm, d), _const_spec((1, d)), _const_spec((bsz, d)),
                  _const_spec((bsz, d)), _const_spec(w_qkv.shape), _row_spec(tm, LANES),
                  _row_spec(tm, LANES)],
        out_specs=[batch_major] * 3,
        out_shape=[jax.ShapeDtypeStruct((bsz, n // bsz, d), BF16)] * 3,
        scratch_shapes=_relayout_scratch(tm, d),
        compiler_params=_params("parallel"),
        name="qkv_proj",
    )(x, g, sh, sc, w_qkv, cos, sin)


def _attn_kernel(q_ref, k_ref, v_ref, lq1_ref, lk1_ref, lq2_ref, lk2_ref, sg_ref, o_ref,
                 kt_ref, vext_ref, qs_ref, m_ref, acc_ref, s_refs, p_refs, a_refs, *,
                 lambda_init, tq):
    seq = q_ref.shape[0]
    vext_ref[:, :DA_V_DIM] = v_ref[...]
    col = lax.broadcasted_iota(jnp.int32, (seq, DA_V_DIM), 1)
    vext_ref[:, DA_V_DIM:] = jnp.where(col == 0, 1.0, 0.0).astype(BF16)
    tb = min(seq, ROW_TILE)
    for blk in range(seq // tb):
        kt_ref[:, blk * tb:(blk + 1) * tb] = (
            k_ref[blk * tb:(blk + 1) * tb, :].astype(F32).T.astype(BF16))

    lam = (jnp.exp(jnp.sum(lq1_ref[...] * lk1_ref[...], keepdims=True))
           - jnp.exp(jnp.sum(lq2_ref[...] * lk2_ref[...], keepdims=True)) + lambda_init)

    lane = lax.broadcasted_iota(jnp.int32, (tq, 2 * DA_HEAD_DIM), 1)
    first = lane < DA_HEAD_DIM
    rows = 2 * tq
    tk = tq // 2

    def row_ranges(diag):
        if not diag:
            return [(0, rows)]
        return [(comp * tq + diag * tk, (comp + 1) * tq) for comp in range(2)]

    def scores(t, slot, diag=None):
        kt = kt_ref[:, pl.ds(pl.multiple_of(t * tk, tk), tk)]
        for r0, r1 in row_ranges(diag):
            s_refs[slot][r0:r1, :] = _dot(qs_ref[r0:r1, :], kt)

    def softmax(slot, diag):
        for r0, r1 in row_ranges(diag):
            for i0 in range(r0, r1, ATTN_SM_ROWS):
                rs = pl.ds(i0, ATTN_SM_ROWS)
                s = s_refs[slot][rs, :]
                if diag is not None:
                    qrow = i0 % tq + lax.broadcasted_iota(jnp.int32, s.shape, 0)
                    key = diag * tk + lax.broadcasted_iota(jnp.int32, s.shape, 1)
                    s = jnp.where(key < (qrow // CHUNK + 1) * CHUNK, s, NEG_BIG)
                parts = [s[:, j * LANES:(j + 1) * LANES] for j in range(tk // LANES)]
                m_prev = m_ref[rs, :]
                m_new = jnp.maximum(
                    m_prev, jnp.max(functools.reduce(jnp.maximum, parts), axis=-1, keepdims=True))
                p_refs[slot][rs, :] = jnp.concatenate(
                    [jnp.exp(pj - m_new) for pj in parts], axis=1).astype(BF16)
                a_refs[slot][rs, :] = jnp.exp(m_prev - m_new)
                m_ref[rs, :] = m_new

    def values(t, slot, diag=None):
        vb = vext_ref[pl.ds(pl.multiple_of(t * tk, tk), tk), :]
        for r0, r1 in row_ranges(diag):
            alpha = a_refs[slot][r0:r1, :]
            acc_ref[r0:r1, :] = (jnp.concatenate([alpha, alpha], axis=1) * acc_ref[r0:r1, :]
                                 + _dot(p_refs[slot][r0:r1, :], vb))

    def q_block(qi, carry):
        q0 = pl.multiple_of(qi * tq, tq)
        qb = q_ref[pl.ds(q0, tq), :]
        zero = jnp.zeros_like(qb)
        qs_ref[:tq, :] = jnp.where(first, qb, zero)
        qs_ref[tq:, :] = jnp.where(first, zero, qb)
        acc_ref[...] = jnp.zeros_like(acc_ref)
        m_ref[...] = jnp.full(m_ref.shape, NEG_BIG, F32)
        p_refs[1][...] = jnp.zeros_like(p_refs[1])
        a_refs[1][...] = jnp.ones_like(a_refs[1])

        def pair(u, last):
            t = 2 * u
            scores(t + 1, 1, 1 if last else None)
            softmax(0, 0 if last else None)
            values(jnp.maximum(t - 1, 0), 1)
            if not last:
                scores(t + 2, 0)
            softmax(1, 1 if last else None)
            values(t, 0)

        def full_pair(u, carry):
            pair(u, False)
            return carry

        scores(0, 0)
        lax.fori_loop(0, qi, full_pair, 0)
        pair(qi, True)
        values(2 * qi + 1, 1, 1)

        for i in range(tq // ATTN_SM_ROWS):
            o = []
            for comp in range(2):
                acc = acc_ref[pl.ds(comp * tq + i * ATTN_SM_ROWS, ATTN_SM_ROWS), :]
                o.append(acc[:, :DA_V_DIM] / acc[:, DA_V_DIM:DA_V_DIM + 1])
            od = _rms(o[0] - lam * o[1], sg_ref[...]) * (1.0 - lambda_init)
            o_ref[pl.ds(q0 + i * ATTN_SM_ROWS, ATTN_SM_ROWS), :] = od.astype(BF16)
        return carry

    lax.fori_loop(0, seq // tq, q_block, 0)


def _diff_attention(q, k, v, lq1, lk1, lq2, lk2, subln_g, lambda_init):
    bsz, seq, d = q.shape
    tq = min(seq, ATTN_TQ)
    rows, tk = 2 * tq, tq // 2
    head = pl.BlockSpec((None, seq, DA_V_DIM), lambda b, h: (b, 0, h))
    vec = _const_spec((1, DA_HEAD_DIM))
    return pl.pallas_call(
        functools.partial(_attn_kernel, lambda_init=lambda_init, tq=tq),
        grid=(bsz, d // DA_V_DIM),
        in_specs=[head, head, head, vec, vec, vec, vec, _const_spec((1, DA_V_DIM))],
        out_specs=head,
        out_shape=jax.ShapeDtypeStruct((bsz, seq, d), BF16),
        scratch_shapes=[pltpu.VMEM((2 * DA_HEAD_DIM, seq), BF16),
                        pltpu.VMEM((seq, 2 * DA_V_DIM), BF16),
                        pltpu.VMEM((rows, 2 * DA_HEAD_DIM), BF16),
                        pltpu.VMEM((rows, LANES), F32),
                        pltpu.VMEM((rows, 2 * DA_V_DIM), F32),
                        [pltpu.VMEM((rows, tk), F32)] * 2,
                        [pltpu.VMEM((rows, tk), BF16)] * 2,
                        [pltpu.VMEM((rows, LANES), F32)] * 2],
        compiler_params=_params("parallel", "parallel"),
        name="diff_attention",
    )(q, k, v, lq1.reshape(1, -1), lk1.reshape(1, -1), lq2.reshape(1, -1), lk2.reshape(1, -1),
      subln_g.reshape(1, -1))


def _oproj_kernel(o_ref, x_ref, w_ref, g_ref, gate_ref, out_ref, *tmp_refs):
    o = _get_time_major(o_ref, tmp_refs).astype(BF16)
    out_ref[...] = _gated_residual(x_ref[...], _dot(o, w_ref[...]), g_ref[...], gate_ref[...])


def _out_proj(o, x, w_o, g, gate):
    n, d = x.shape
    bsz = gate.shape[0]
    tm = min(n, ROW_TILE)
    return pl.pallas_call(
        _oproj_kernel,
        grid=(n // tm,),
        in_specs=[pl.BlockSpec((bsz, tm // bsz, d), lambda i: (0, i, 0)), _row_spec(tm, d),
                  _const_spec(w_o.shape), _const_spec((1, d)), _const_spec((bsz, d))],
        out_specs=_row_spec(tm, d),
        out_shape=jax.ShapeDtypeStruct(x.shape, F32),
        scratch_shapes=_relayout_scratch(tm, d),
        compiler_params=_params("parallel"),
        name="attn_out_proj",
    )(o, x, w_o, g, gate)


def _ffn_kernel(x_ref, xprev_ref, g2_ref, sh_ref, sc_ref, gate_ref, g3_ref, win_ref, cw_ref, cb_ref,
                wout_ref, out_ref, h_ref, acc_ref, u_refs, g_ref, *tmp_refs):
    tm, d = x_ref.shape
    f = wout_ref.shape[0]
    bsz = sh_ref.shape[0]
    halo = xprev_ref.shape[0]
    tf = FFN_TF
    n_tiles = f // tf
    g2, sh, sc = g2_ref[...], sh_ref[...], sc_ref[...]
    h_ref[:halo, :] = _modulated_norm(xprev_ref[...], g2, sh, sc).astype(BF16)
    h_ref[halo:, :] = _modulated_norm(x_ref[...], g2, sh, sc).astype(BF16)
    keep_halo = jnp.where(pl.program_id(0) > 0, 1.0, 0.0)
    acc_ref[...] = jnp.zeros_like(acc_ref)

    def up(i, slot):
        c0 = pl.multiple_of(i * tf, tf)
        hb = h_ref[...]
        for half, col in enumerate((pl.ds(c0, tf), pl.ds(f + c0, tf))):
            u_refs[slot][:, half * tf:(half + 1) * tf] = _dot(hb, win_ref[:, col])
        u_refs[slot][:halo, :] = u_refs[slot][:halo, :] * keep_halo

    def gate(i, slot):
        c0 = pl.multiple_of(i * tf, tf)
        halves = []
        for half, col in enumerate((pl.ds(c0, tf), pl.ds(f + c0, tf))):
            w = cw_ref[:, col].astype(BF16)
            out = cb_ref[:, col].astype(BF16)
            for j in range(CONV_WIDTH):
                u = u_refs[slot][pl.ds(j * bsz, tm), half * tf:(half + 1) * tf]
                out = out + w[j:j + 1] * u.astype(BF16)
            halves.append(out)
        g_ref[:, slot * tf:(slot + 1) * tf] = jax.nn.gelu(halves[0]) * halves[1]

    def down(i, ntile):
        c0 = pl.multiple_of(i * tf, tf)
        acc_ref[...] += _dot(g_ref[:, :ntile * tf], wout_ref[pl.ds(c0, ntile * tf), :])

    def pair(v, carry):
        i = 2 * v
        up(i + 1, 1)
        gate(i, 0)
        up(i + 2, 0)
        gate(i + 1, 1)
        down(i, 2)
        return carry

    assert n_tiles % 2 == 1
    up(0, 0)
    lax.fori_loop(0, n_tiles // 2, pair, 0)
    gate(n_tiles - 1, 0)
    down(n_tiles - 1, 1)
    res = _gated_residual(x_ref[...], acc_ref[...], g3_ref[...], gate_ref[...])
    if tmp_refs:
        for j, tmp_ref in enumerate(tmp_refs):
            _put_batch_major(out_ref, j, res[:, j * LANES:(j + 1) * LANES], tmp_ref)
    else:
        out_ref[...] = res


def _conv_ffn(x, g2, sh, sc, gate, g3, w_in, conv_w, conv_b, w_out, batch_major_out=False):
    n, d = x.shape
    f = w_out.shape[0]
    bsz = sh.shape[0]
    tm = min(n, ROW_TILE)
    halo = (CONV_WIDTH - 1) * bsz
    prev = pl.BlockSpec((halo, d), lambda i: (jnp.maximum(i * (tm // halo) - 1, 0), 0))
    vec = _const_spec((bsz, d))
    if batch_major_out:
        out_spec = pl.BlockSpec((bsz, tm // bsz, d), lambda i: (0, i, 0))
        out_shape, relayout = (bsz, n // bsz, d), _relayout_scratch(tm, d)
    else:
        out_spec, out_shape, relayout = _row_spec(tm, d), (n, d), []
    return pl.pallas_call(
        _ffn_kernel,
        grid=(n // tm,),
        in_specs=[_row_spec(tm, d), prev, _const_spec((1, d)), vec, vec, vec, _const_spec((1, d)),
                  _const_spec(w_in.shape), _const_spec(conv_w.shape), _const_spec((1, 2 * f)),
                  _const_spec(w_out.shape)],
        out_specs=out_spec,
        out_shape=jax.ShapeDtypeStruct(out_shape, F32),
        scratch_shapes=[pltpu.VMEM((tm + halo, d), BF16),
                        pltpu.VMEM((tm, d), F32),
                        [pltpu.VMEM((tm + halo, 2 * FFN_TF), F32)] * 2,
                        pltpu.VMEM((tm, 2 * FFN_TF), BF16)] + relayout,
        compiler_params=_params("parallel"),
        name="conv_ffn",
    )(x, x, g2, sh, sc, gate, g3, w_in, conv_w, conv_b.reshape(1, 2 * f), w_out)


def kernel(x, c, positions, ada_w, ada_b, norm_g, s5_a_re, s5_a_im, s5_log_dt, s5_b_re, s5_b_im,
           s5_c_re, s5_c_im, s5_d, s5_w_glu, da_w_qkv, da_w_o, da_lq1, da_lk1, da_lq2, da_lk2,
           da_subln_g, ffn_w_in, ffn_conv_w, ffn_conv_b, ffn_w_out):
    depth = ada_w.shape[0]
    bsz, seq, d = x.shape
    n = seq * bsz
    mod = _ada_mod(c, ada_w, ada_b).reshape(depth, bsz, 6, d)
    lb_re, lb_im, bb_re, bb_im = _s5_discretise(s5_a_re, s5_a_im, s5_log_dt, s5_b_re, s5_b_im)
    cos, sin = _rope_tables(positions.T.reshape(n, 1))
    for i in range(depth):
        sh_t, sc_t, g_t, sh_c, sc_c, g_c = (mod[i, :, m] for m in range(6))
        gains = norm_g[i].reshape(4, 1, d)
        j = i // 2
        if i % 2 == 0:
            packed = _s5_pack_weights(lb_re[j], lb_im[j], bb_re[j], bb_im[j], s5_c_re[j], s5_c_im[j])
            x = _s5_layer(x, gains[0], sh_t, sc_t, g_t, gains[1], packed, s5_d[j],
                          s5_w_glu[j].astype(BF16))
        else:
            lambda_init = 0.8 - 0.6 * math.exp(-0.3 * i)
            q, k, v = _qkv_proj(x, gains[0], sh_t, sc_t, da_w_qkv[j].astype(BF16), cos, sin)
            o = _diff_attention(q, k, v, da_lq1[j], da_lk1[j], da_lq2[j], da_lk2[j],
                                da_subln_g[j], lambda_init)
            x = _out_proj(o, x, da_w_o[j].astype(BF16), gains[1], g_t)
        x = _conv_ffn(x, gains[2], sh_c, sc_c, g_c, gains[3], ffn_w_in[i].astype(BF16),
                      ffn_conv_w[i], ffn_conv_b[i], ffn_w_out[i].astype(BF16),
                      batch_major_out=(i == depth - 1))
    return x
```

```python
import functools
import math

import jax
import jax.numpy as jnp
from jax import lax
from jax.experimental import pallas as pl
from jax.experimental.pallas import tpu as pltpu

F32 = jnp.float32
BF16 = jnp.bfloat16

EPS = 1e-6
CHUNK = 64
S5_GROUP = 16
S5_STATE = 64
S5_LAMBDA_RE_MAX = -1e-4
DA_HEADS = 8
DA_HEAD_DIM = 64
DA_V_DIM = 2 * DA_HEAD_DIM
ROPE_THETA = 10000.0
CONV_WIDTH = 3

LANES = 128
MXU_DIM = 256
VMEM_LIMIT_BYTES = 56 * 1024 * 1024

S5_PACK_GROUPS = MXU_DIM // S5_GROUP
S5_PACK_STATES = S5_PACK_GROUPS * S5_STATE
S5_SCAN_LANES = 4 * LANES
S5_FRAMES = 128

ROW_TILE = 1024
ATTN_TQ = 512
ATTN_SM_ROWS = 128
FFN_TF = 256
NEG_BIG = -1e30


def _params(*sem):
    return pltpu.CompilerParams(dimension_semantics=sem, vmem_limit_bytes=VMEM_LIMIT_BYTES)


def _const_spec(shape):
    nd = len(shape)
    return pl.BlockSpec(shape, lambda *_: (0,) * nd, pipeline_mode=pl.Buffered(1))


def _row_spec(tm, width):
    return pl.BlockSpec((tm, width), lambda i: (i, 0))


def _rms(x, g):
    ms = jnp.mean(x * x, axis=-1, keepdims=True)
    return x * lax.rsqrt(ms + EPS) * g


def _per_seq(x, bsz, fn):
    rows, d = x.shape
    if bsz == 1:
        return fn(x)
    return fn(x.reshape(rows // bsz, bsz, d)).reshape(rows, d)


def _modulated_norm(x, g, sh, sc):
    return _per_seq(_rms(x, g), sh.shape[0], lambda y: y * (1.0 + sc) + sh)


def _gated_residual(x, y, g, gate):
    return x + _per_seq(_rms(y, g), gate.shape[0], lambda r: gate * r)


def _dot(a, b):
    return jnp.dot(a, b, preferred_element_type=F32)


def _relayout_scratch(rows, d):
    return [pltpu.VMEM((rows, LANES), F32)] * (d // LANES)


def _put_batch_major(dst_ref, j, val, tmp_ref):
    bsz, tt, _ = dst_ref.shape
    tmp_ref[...] = val
    for b in range(bsz):
        dst_ref[b, :, j * LANES:(j + 1) * LANES] = (
            tmp_ref[pl.ds(b, tt, stride=bsz), :].astype(dst_ref.dtype))


def _get_time_major(src_ref, tmp_refs):
    bsz, tt, _ = src_ref.shape
    for j, tmp_ref in enumerate(tmp_refs):
        for b in range(bsz):
            tmp_ref[pl.ds(b, tt, stride=bsz), :] = src_ref[b, :, j * LANES:(j + 1) * LANES].astype(F32)
    return jnp.concatenate([t[...] for t in tmp_refs], axis=1)


def _ada_kernel(c_ref, w_ref, b_ref, o_ref):
    c = c_ref[...]
    cond = (c * jax.nn.sigmoid(c)).astype(BF16)
    o_ref[...] = _dot(cond, w_ref[...].astype(BF16)) + b_ref[...]


def _ada_mod(c, ada_w, ada_b):
    depth, d, n = ada_w.shape
    bsz = c.shape[0]
    tn = n // 4
    return pl.pallas_call(
        _ada_kernel,
        grid=(depth, n // tn),
        in_specs=[
            pl.BlockSpec((bsz, d), lambda i, j: (0, 0)),
            pl.BlockSpec((None, d, tn), lambda i, j: (i, 0, j)),
            pl.BlockSpec((None, 1, tn), lambda i, j: (i, 0, j)),
        ],
        out_specs=pl.BlockSpec((None, bsz, tn), lambda i, j: (i, 0, j)),
        out_shape=jax.ShapeDtypeStruct((depth, bsz, n), F32),
        compiler_params=_params("parallel", "parallel"),
        name="ada_mod",
    )(c, ada_w, ada_b.reshape(depth, 1, n))


def _s5_disc_kernel(are_ref, aim_ref, ldt_ref, br_ref, bi_ref, lbr_ref, lbi_ref, bbr_ref, bbi_ref):
    lam_re = jnp.minimum(are_ref[...], S5_LAMBDA_RE_MAX)
    lam_im = aim_ref[...]
    dt = jnp.exp(ldt_ref[...])
    dre, dimg = lam_re * dt, lam_im * dt
    mag = jnp.exp(dre)
    lb_re, lb_im = mag * jnp.cos(dimg), mag * jnp.sin(dimg)
    den = lam_re * lam_re + lam_im * lam_im
    nr = lb_re - 1.0
    f_re = (nr * lam_re + lb_im * lam_im) / den
    f_im = (lb_im * lam_re - nr * lam_im) / den
    br, bi = br_ref[...], bi_ref[...]
    lbr_ref[...] = lb_re
    lbi_ref[...] = lb_im
    bbr_ref[...] = f_re * br - f_im * bi
    bbi_ref[...] = f_re * bi + f_im * br


def _s5_discretise(a_re, a_im, log_dt, b_re, b_im):
    n, g, p = a_re.shape
    cg = b_re.shape[-1]
    vec = pl.BlockSpec((None, g, 1, p), lambda i: (i, 0, 0, 0))
    mat = pl.BlockSpec((None, g, cg, p), lambda i: (i, 0, 0, 0))
    return pl.pallas_call(
        _s5_disc_kernel,
        grid=(n,),
        in_specs=[vec, vec, pl.BlockSpec((None, g, 1, 1), lambda i: (i, 0, 0, 0)), mat, mat],
        out_specs=[vec, vec, mat, mat],
        out_shape=[jax.ShapeDtypeStruct((n, g, 1, p), F32)] * 2
        + [jax.ShapeDtypeStruct((n, g, cg, p), F32)] * 2,
        compiler_params=_params("parallel"),
        name="s5_discretise",
    )(a_re.reshape(n, g, 1, p), a_im.reshape(n, g, 1, p), log_dt.reshape(n, g, 1, 1),
      jnp.swapaxes(b_re, -1, -2), jnp.swapaxes(b_im, -1, -2))


def _s5_pack_weights(lb_re, lb_im, bb_re, bb_im, c_re, c_im):
    g, cg, p = bb_re.shape
    gp = S5_PACK_GROUPS
    npack = g // gp
    eye = jnp.eye(gp, dtype=F32)

    def in_proj(bb):
        return jnp.einsum("kgcp,gh->kgchp", bb.reshape(npack, gp, cg, p), eye).reshape(
            npack, gp * cg, gp * p)

    def out_proj(c):
        return jnp.einsum("kgcp,gh->kgphc", c.reshape(npack, gp, cg, p), eye).reshape(
            npack, gp * p, gp * cg)

    wb = jnp.concatenate([in_proj(bb_re), in_proj(bb_im)], axis=-1).astype(BF16)
    return (wb, out_proj(c_re).astype(BF16), out_proj(c_im).astype(BF16),
            lb_re.reshape(npack, 1, gp * p), lb_im.reshape(npack, 1, gp * p))


def _s5_kernel(x_ref, g0_ref, sh_ref, sc_ref, gate_ref, g1_ref, wb_ref, lbr_ref, lbi_ref,
               wcr_ref, wci_ref, dsk_ref, wglu_ref, out_ref, h_ref, u_ref, bu_refs, y_ref,
               *relayout):
    if relayout:
        xtm_ref, *tmp_refs = relayout
        xtm_ref[...] = _get_time_major(x_ref, tmp_refs)
        x_ref = xtm_ref
    rows, d = x_ref.shape
    bsz = sh_ref.shape[0]
    npack = wb_ref.shape[0]
    ns = S5_PACK_STATES

    @pl.when(pl.program_id(0) == 0)
    def _():
        h_ref[...] = jnp.zeros_like(h_ref)

    u_ref[...] = _modulated_norm(x_ref[...], g0_ref[...], sh_ref[...], sc_ref[...])

    for k in range(npack):
        c0 = k * MXU_DIM
        bu_ref = bu_refs[k % 2]
        bu_ref[...] = _dot(u_ref[:, c0:c0 + MXU_DIM].astype(BF16), wb_ref[k])
        for s in range(ns // S5_SCAN_LANES):
            re = pl.ds(s * S5_SCAN_LANES, S5_SCAN_LANES)
            im = pl.ds(ns + s * S5_SCAN_LANES, S5_SCAN_LANES)
            lbr = jnp.broadcast_to(lbr_ref[k, :, re], (bsz, S5_SCAN_LANES))
            lbi = jnp.broadcast_to(lbi_ref[k, :, re], (bsz, S5_SCAN_LANES))
            hr, hi = h_ref[k, :, re], h_ref[k, :, im]
            for t in range(rows // bsz):
                frame = pl.ds(t * bsz, bsz)
                hr, hi = (lbr * hr - lbi * hi + bu_ref[frame, re],
                          lbr * hi + lbi * hr + bu_ref[frame, im])
                bu_ref[frame, re] = hr
                bu_ref[frame, im] = hi
            h_ref[k, :, re] = hr
            h_ref[k, :, im] = hi
        y_ref[:, c0:c0 + MXU_DIM] = (_dot(bu_ref[:, :ns].astype(BF16), wcr_ref[k])
                                     - _dot(bu_ref[:, ns:].astype(BF16), wci_ref[k]))

    y = y_ref[...] + dsk_ref[...] * u_ref[...]
    o = _dot(jax.nn.gelu(y).astype(BF16), wglu_ref[...])
    mix = o[:, :d] * jax.nn.sigmoid(o[:, d:])
    out_ref[...] = _gated_residual(x_ref[...], mix, g1_ref[...], gate_ref[...])


def _s5_layer(x, g0, sh, sc, gate, g1, packed, d_skip, w_glu):
    bsz, d = sh.shape
    n = x.size // d
    wb, wcr, wci, lbr, lbi = packed
    npack = wb.shape[0]
    rows = S5_FRAMES * bsz
    if x.ndim == 3:
        x_spec = pl.BlockSpec((bsz, S5_FRAMES, d), lambda t: (0, t, 0))
        relayout = [pltpu.VMEM((rows, d), F32)] + _relayout_scratch(rows, d)
    else:
        x_spec, relayout = _row_spec(rows, d), []
    return pl.pallas_call(
        _s5_kernel,
        grid=(n // rows,),
        in_specs=[x_spec, _const_spec((1, d)), _const_spec((bsz, d)),
                  _const_spec((bsz, d)), _const_spec((bsz, d)), _const_spec((1, d)),
                  _const_spec(wb.shape), _const_spec(lbr.shape), _const_spec(lbi.shape),
                  _const_spec(wcr.shape), _const_spec(wci.shape), _const_spec((1, d)),
                  _const_spec(w_glu.shape)],
        out_specs=_row_spec(rows, d),
        out_shape=jax.ShapeDtypeStruct((n, d), F32),
        scratch_shapes=[pltpu.VMEM((npack, bsz, 2 * S5_PACK_STATES), F32),
                        pltpu.VMEM((rows, d), F32),
                        [pltpu.VMEM((rows, 2 * S5_PACK_STATES), F32)] * 2,
                        pltpu.VMEM((rows, d), F32)] + relayout,
        compiler_params=_params("arbitrary"),
        name="s5_layer",
    )(x, g0, sh, sc, gate, g1, wb, lbr, lbi, wcr, wci, d_skip.reshape(1, d), w_glu)


def _rope_kernel(pos_ref, inv_ref, cos_ref, sin_ref):
    ang = pos_ref[...].astype(F32) * inv_ref[...]
    lane = lax.broadcasted_iota(jnp.int32, ang.shape, 1)
    first_half = lane % DA_HEAD_DIM < DA_HEAD_DIM // 2
    cos_ref[...] = jnp.cos(ang)
    sin_ref[...] = jnp.where(first_half, -1.0, 1.0) * jnp.sin(ang)


def _rope_tables(pos_rows):
    n = pos_rows.shape[0]
    half = DA_HEAD_DIM // 2
    inv = ROPE_THETA ** (-jnp.arange(half, dtype=F32) / half)
    inv = jnp.tile(inv, LANES // half).reshape(1, LANES)
    tm = min(n, ROW_TILE)
    return pl.pallas_call(
        _rope_kernel,
        grid=(n // tm,),
        in_specs=[_row_spec(tm, 1), pl.BlockSpec((1, LANES), lambda i: (0, 0))],
        out_specs=[_row_spec(tm, LANES)] * 2,
        out_shape=[jax.ShapeDtypeStruct((n, LANES), F32)] * 2,
        compiler_params=_params("parallel"),
        name="rope_tables",
    )(pos_rows, inv)


def _qkv_kernel(x_ref, g_ref, sh_ref, sc_ref, w_ref, cos_ref, sin_ref, q_ref, k_ref, v_ref,
                *tmp_refs):
    d = x_ref.shape[-1]
    h = _modulated_norm(x_ref[...], g_ref[...], sh_ref[...], sc_ref[...]).astype(BF16)
    cos, sin = cos_ref[...], sin_ref[...]
    lane = lax.broadcasted_iota(jnp.int32, cos.shape, 1)
    half = DA_HEAD_DIM // 2
    first_half = lane % DA_HEAD_DIM < half

    for src, dst, scale in ((0, q_ref, DA_HEAD_DIM ** -0.5), (d, k_ref, 1.0)):
        t = _dot(h, w_ref[:, src:src + d])
        for j in range(d // LANES):
            tj = t[:, j * LANES:(j + 1) * LANES]
            rot = jnp.where(first_half, pltpu.roll(tj, LANES - half, 1), pltpu.roll(tj, half, 1))
            _put_batch_major(dst, j, (tj * cos + rot * sin) * scale, tmp_refs[j])
    t = _dot(h, w_ref[:, 2 * d:])
    for j in range(d // LANES):
        _put_batch_major(v_ref, j, t[:, j * LANES:(j + 1) * LANES], tmp_refs[j])


def _qkv_proj(x, g, sh, sc, w_qkv, cos, sin):
    n, d = x.shape
    bsz = sh.shape[0]
    tm = min(n, ROW_TILE)
    batch_major = pl.BlockSpec((bsz, tm // bsz, d), lambda i: (0, i, 0))
    return pl.pallas_call(
        _qkv_kernel,
        grid=(n // tm,),
        in_specs=[_row_spec(tm, d), _const_spec((1, d)), _const_spec((bsz, d)),
                  _const_spec((bsz, d)), _const_spec(w_qkv.shape), _row_spec(tm, LANES),
                  _row_spec(tm, LANES)],
        out_specs=[batch_major] * 3,
        out_shape=[jax.ShapeDtypeStruct((bsz, n // bsz, d), BF16)] * 3,
        scratch_shapes=_relayout_scratch(tm, d),
        compiler_params=_params("parallel"),
        name="qkv_proj",
    )(x, g, sh, sc, w_qkv, cos, sin)


def _attn_kernel(q_ref, k_ref, v_ref, lq1_ref, lk1_ref, lq2_ref, lk2_ref, sg_ref, o_ref,
                 kt_ref, vext_ref, qs_ref, m_ref, acc_ref, s_refs, p_refs, a_refs, *,
                 lambda_init, tq):
    seq = q_ref.shape[0]
    vext_ref[:, :DA_V_DIM] = v_ref[...]
    col = lax.broadcasted_iota(jnp.int32, (seq, DA_V_DIM), 1)
    vext_ref[:, DA_V_DIM:] = jnp.where(col == 0, 1.0, 0.0).astype(BF16)
    tb = min(seq, ROW_TILE)
    for blk in range(seq // tb):
        kt_ref[:, blk * tb:(blk + 1) * tb] = (
            k_ref[blk * tb:(blk + 1) * tb, :].astype(F32).T.astype(BF16))

    lam = (jnp.exp(jnp.sum(lq1_ref[...] * lk1_ref[...], keepdims=True))
           - jnp.exp(jnp.sum(lq2_ref[...] * lk2_ref[...], keepdims=True)) + lambda_init)

    lane = lax.broadcasted_iota(jnp.int32, (tq, 2 * DA_HEAD_DIM), 1)
    first = lane < DA_HEAD_DIM
    rows = 2 * tq
    tk = tq // 2

    def row_ranges(diag):
        if not diag:
            return [(0, rows)]
        return [(comp * tq + diag * tk, (comp + 1) * tq) for comp in range(2)]

    def scores(t, diag):
        kt = kt_ref[:, t * tk:(t + 1) * tk]
        for r0, r1 in row_ranges(diag):
            s_refs[t % 2][r0:r1, :] = _dot(qs_ref[r0:r1, :], kt)

    def softmax(t, diag, first):
        for r0, r1 in row_ranges(diag):
            for i0 in range(r0, r1, ATTN_SM_ROWS):
                rs = pl.ds(i0, ATTN_SM_ROWS)
                s = s_refs[t % 2][rs, :]
                if diag is not None:
                    qrow = i0 % tq + lax.broadcasted_iota(jnp.int32, s.shape, 0)
                    key = diag * tk + lax.broadcasted_iota(jnp.int32, s.shape, 1)
                    s = jnp.where(key < (qrow // CHUNK + 1) * CHUNK, s, NEG_BIG)
                parts = [s[:, j * LANES:(j + 1) * LANES] for j in range(tk // LANES)]
                m_new = jnp.max(functools.reduce(jnp.maximum, parts), axis=-1, keepdims=True)
                if first:
                    m_new = jnp.broadcast_to(m_new, (ATTN_SM_ROWS, LANES))
                else:
                    m_prev = m_ref[rs, :]
                    m_new = jnp.maximum(m_prev, m_new)
                    a_refs[t % 2][rs, :] = jnp.exp(m_prev - m_new)
                p_refs[t % 2][rs, :] = jnp.concatenate(
                    [jnp.exp(pj - m_new) for pj in parts], axis=1).astype(BF16)
                m_ref[rs, :] = m_new

    def values(t, diag, first):
        vb = vext_ref[t * tk:(t + 1) * tk, :]
        for r0, r1 in row_ranges(diag):
            pv = _dot(p_refs[t % 2][r0:r1, :], vb)
            if not first:
                alpha = a_refs[t % 2][r0:r1, :]
                pv = jnp.concatenate([alpha, alpha], axis=1) * acc_ref[r0:r1, :] + pv
            acc_ref[r0:r1, :] = pv

    for qi in range(seq // tq):
        q0 = qi * tq
        qb = q_ref[q0:q0 + tq, :]
        zero = jnp.zeros_like(qb)
        qs_ref[:tq, :] = jnp.where(first, qb, zero)
        qs_ref[tq:, :] = jnp.where(first, zero, qb)

        n_blocks = 2 * qi + 2
        diag = [None] * (n_blocks - 2) + [0, 1]
        scores(0, diag[0])
        for t in range(n_blocks):
            if t + 1 < n_blocks:
                scores(t + 1, diag[t + 1])
            softmax(t, diag[t], t == 0)
            if t >= 1:
                values(t - 1, diag[t - 1], t == 1)
        values(n_blocks - 1, diag[-1], False)

        for i in range(tq // ATTN_SM_ROWS):
            o = []
            for comp in range(2):
                r0 = comp * tq + i * ATTN_SM_ROWS
                acc = acc_ref[r0:r0 + ATTN_SM_ROWS, :]
                o.append(acc[:, :DA_V_DIM] / acc[:, DA_V_DIM:DA_V_DIM + 1])
            od = _rms(o[0] - lam * o[1], sg_ref[...]) * (1.0 - lambda_init)
            r0 = q0 + i * ATTN_SM_ROWS
            o_ref[r0:r0 + ATTN_SM_ROWS, :] = od.astype(BF16)


def _diff_attention(q, k, v, lq1, lk1, lq2, lk2, subln_g, lambda_init):
    bsz, seq, d = q.shape
    tq = min(seq, ATTN_TQ)
    rows, tk = 2 * tq, tq // 2
    head = pl.BlockSpec((None, seq, DA_V_DIM), lambda b, h: (b, 0, h))
    vec = _const_spec((1, DA_HEAD_DIM))
    return pl.pallas_call(
        functools.partial(_attn_kernel, lambda_init=lambda_init, tq=tq),
        grid=(bsz, d // DA_V_DIM),
        in_specs=[head, head, head, vec, vec, vec, vec, _const_spec((1, DA_V_DIM))],
        out_specs=head,
        out_shape=jax.ShapeDtypeStruct((bsz, seq, d), BF16),
        scratch_shapes=[pltpu.VMEM((2 * DA_HEAD_DIM, seq), BF16),
                        pltpu.VMEM((seq, 2 * DA_V_DIM), BF16),
                        pltpu.VMEM((rows, 2 * DA_HEAD_DIM), BF16),
                        pltpu.VMEM((rows, LANES), F32),
                        pltpu.VMEM((rows, 2 * DA_V_DIM), F32),
                        [pltpu.VMEM((rows, tk), F32)] * 2,
                        [pltpu.VMEM((rows, tk), BF16)] * 2,
                        [pltpu.VMEM((rows, LANES), F32)] * 2],
        compiler_params=_params("parallel", "parallel"),
        name="diff_attention",
    )(q, k, v, lq1.reshape(1, -1), lk1.reshape(1, -1), lq2.reshape(1, -1), lk2.reshape(1, -1),
      subln_g.reshape(1, -1))


def _oproj_kernel(o_ref, x_ref, w_ref, g_ref, gate_ref, out_ref, *tmp_refs):
    o = _get_time_major(o_ref, tmp_refs).astype(BF16)
    out_ref[...] = _gated_residual(x_ref[...], _dot(o, w_ref[...]), g_ref[...], gate_ref[...])


def _out_proj(o, x, w_o, g, gate):
    n, d = x.shape
    bsz = gate.shape[0]
    tm = min(n, ROW_TILE)
    return pl.pallas_call(
        _oproj_kernel,
        grid=(n // tm,),
        in_specs=[pl.BlockSpec((bsz, tm // bsz, d), lambda i: (0, i, 0)), _row_spec(tm, d),
                  _const_spec(w_o.shape), _const_spec((1, d)), _const_spec((bsz, d))],
        out_specs=_row_spec(tm, d),
        out_shape=jax.ShapeDtypeStruct(x.shape, F32),
        scratch_shapes=_relayout_scratch(tm, d),
        compiler_params=_params("parallel"),
        name="attn_out_proj",
    )(o, x, w_o, g, gate)


def _ffn_kernel(x_ref, xprev_ref, g2_ref, sh_ref, sc_ref, gate_ref, g3_ref, win_ref, cw_ref, cb_ref,
                wout_ref, out_ref, h_ref, acc_ref, u_refs, g_ref, *tmp_refs):
    tm, d = x_ref.shape
    f = wout_ref.shape[0]
    bsz = sh_ref.shape[0]
    halo = xprev_ref.shape[0]
    tf = FFN_TF
    n_tiles = f // tf
    g2, sh, sc = g2_ref[...], sh_ref[...], sc_ref[...]
    h_ref[:halo, :] = _modulated_norm(xprev_ref[...], g2, sh, sc).astype(BF16)
    h_ref[halo:, :] = _modulated_norm(x_ref[...], g2, sh, sc).astype(BF16)
    keep_halo = jnp.where(pl.program_id(0) > 0, 1.0, 0.0)
    acc_ref[...] = jnp.zeros_like(acc_ref)

    def up(i, slot):
        c0 = i * tf
        hb = h_ref[...]
        for half, col in enumerate((pl.ds(c0, tf), pl.ds(f + c0, tf))):
            u_refs[slot][:, half * tf:(half + 1) * tf] = _dot(hb, win_ref[:, col])
        u_refs[slot][:halo, :] = u_refs[slot][:halo, :] * keep_halo

    def gate(i, slot):
        c0 = i * tf
        halves = []
        for half, col in enumerate((pl.ds(c0, tf), pl.ds(f + c0, tf))):
            w = cw_ref[:, col].astype(BF16)
            out = cb_ref[:, col].astype(BF16)
            for j in range(CONV_WIDTH):
                u = u_refs[slot][pl.ds(j * bsz, tm), half * tf:(half + 1) * tf]
                out = out + w[j:j + 1] * u.astype(BF16)
            halves.append(out)
        g_ref[:, slot * tf:(slot + 1) * tf] = jax.nn.gelu(halves[0]) * halves[1]

    def down(i, ntile):
        c0 = i * tf
        acc_ref[...] += _dot(g_ref[:, :ntile * tf], wout_ref[pl.ds(c0, ntile * tf), :])

    assert n_tiles % 2 == 1
    up(0, 0)
    for i in range(0, n_tiles - 1, 2):
        up(i + 1, 1)
        gate(i, 0)
        up(i + 2, 0)
        gate(i + 1, 1)
        down(i, 2)
    gate(n_tiles - 1, 0)
    down(n_tiles - 1, 1)
    res = _gated_residual(x_ref[...], acc_ref[...], g3_ref[...], gate_ref[...])
    if tmp_refs:
        for j, tmp_ref in enumerate(tmp_refs):
            _put_batch_major(out_ref, j, res[:, j * LANES:(j + 1) * LANES], tmp_ref)
    else:
        out_ref[...] = res


def _conv_ffn(x, g2, sh, sc, gate, g3, w_in, conv_w, conv_b, w_out, batch_major_out=False):
    n, d = x.shape
    f = w_out.shape[0]
    bsz = sh.shape[0]
    tm = min(n, ROW_TILE)
    halo = (CONV_WIDTH - 1) * bsz
    prev = pl.BlockSpec((halo, d), lambda i: (jnp.maximum(i * (tm // halo) - 1, 0), 0))
    vec = _const_spec((bsz, d))
    if batch_major_out:
        out_spec = pl.BlockSpec((bsz, tm // bsz, d), lambda i: (0, i, 0))
        out_shape, relayout = (bsz, n // bsz, d), _relayout_scratch(tm, d)
    else:
        out_spec, out_shape, relayout = _row_spec(tm, d), (n, d), []
    return pl.pallas_call(
        _ffn_kernel,
        grid=(n // tm,),
        in_specs=[_row_spec(tm, d), prev, _const_spec((1, d)), vec, vec, vec, _const_spec((1, d)),
                  _const_spec(w_in.shape), _const_spec(conv_w.shape), _const_spec((1, 2 * f)),
                  _const_spec(w_out.shape)],
        out_specs=out_spec,
        out_shape=jax.ShapeDtypeStruct(out_shape, F32),
        scratch_shapes=[pltpu.VMEM((tm + halo, d), BF16),
                        pltpu.VMEM((tm, d), F32),
                        [pltpu.VMEM((tm + halo, 2 * FFN_TF), F32)] * 2,
                        pltpu.VMEM((tm, 2 * FFN_TF), BF16)] + relayout,
        compiler_params=_params("parallel"),
        name="conv_ffn",
    )(x, x, g2, sh, sc, gate, g3, w_in, conv_w, conv_b.reshape(1, 2 * f), w_out)


def kernel(x, c, positions, ada_w, ada_b, norm_g, s5_a_re, s5_a_im, s5_log_dt, s5_b_re, s5_b_im,
           s5_c_re, s5_c_im, s5_d, s5_w_glu, da_w_qkv, da_w_o, da_lq1, da_lk1, da_lq2, da_lk2,
           da_subln_g, ffn_w_in, ffn_conv_w, ffn_conv_b, ffn_w_out):
    depth = ada_w.shape[0]
    bsz, seq, d = x.shape
    n = seq * bsz
    mod = _ada_mod(c, ada_w, ada_b).reshape(depth, bsz, 6, d)
    lb_re, lb_im, bb_re, bb_im = _s5_discretise(s5_a_re, s5_a_im, s5_log_dt, s5_b_re, s5_b_im)
    cos, sin = _rope_tables(positions.T.reshape(n, 1))
    for i in range(depth):
        sh_t, sc_t, g_t, sh_c, sc_c, g_c = (mod[i, :, m] for m in range(6))
        gains = norm_g[i].reshape(4, 1, d)
        j = i // 2
        if i % 2 == 0:
            packed = _s5_pack_weights(lb_re[j], lb_im[j], bb_re[j], bb_im[j], s5_c_re[j], s5_c_im[j])
            x = _s5_layer(x, gains[0], sh_t, sc_t, g_t, gains[1], packed, s5_d[j],
                          s5_w_glu[j].astype(BF16))
        else:
            lambda_init = 0.8 - 0.6 * math.exp(-0.3 * i)
            q, k, v = _qkv_proj(x, gains[0], sh_t, sc_t, da_w_qkv[j].astype(BF16), cos, sin)
            o = _diff_attention(q, k, v, da_lq1[j], da_lk1[j], da_lq2[j], da_lk2[j],
                                da_subln_g[j], lambda_init)
            x = _out_proj(o, x, da_w_o[j].astype(BF16), gains[1], g_t)
        x = _conv_ffn(x, gains[2], sh_c, sc_c, g_c, gains[3], ffn_w_in[i].astype(BF16),
                      ffn_conv_w[i], ffn_conv_b[i], ffn_w_out[i].astype(BF16),
                      batch_major_out=(i == depth - 1))
    return x
```

```python
import functools
import math

import jax
import jax.numpy as jnp
from jax import lax
from jax.experimental import pallas as pl
from jax.experimental.pallas import tpu as pltpu

F32 = jnp.float32
BF16 = jnp.bfloat16

EPS = 1e-6
CHUNK = 64
S5_GROUP = 16
S5_STATE = 64
S5_LAMBDA_RE_MAX = -1e-4
DA_HEADS = 8
DA_HEAD_DIM = 64
DA_V_DIM = 2 * DA_HEAD_DIM
ROPE_THETA = 10000.0
CONV_WIDTH = 3

LANES = 128
MXU_DIM = 256
VMEM_LIMIT_BYTES = 56 * 1024 * 1024

S5_PACK_GROUPS = MXU_DIM // S5_GROUP
S5_PACK_STATES = S5_PACK_GROUPS * S5_STATE
S5_SCAN_LANES = 4 * LANES
S5_FRAMES = 128

ROW_TILE = 1024
ATTN_TQ = 512
ATTN_TK = 1024
ATTN_DIAG_KEYS = 256
ATTN_SM_ROWS = 256
FFN_TF = 256
NEG_BIG = -1e30


def _params(*sem):
    return pltpu.CompilerParams(dimension_semantics=sem, vmem_limit_bytes=VMEM_LIMIT_BYTES)


def _const_spec(shape):
    nd = len(shape)
    return pl.BlockSpec(shape, lambda *_: (0,) * nd, pipeline_mode=pl.Buffered(1))


def _row_spec(tm, width):
    return pl.BlockSpec((tm, width), lambda i: (i, 0))


def _rms(x, g):
    ms = jnp.mean(x * x, axis=-1, keepdims=True)
    return x * lax.rsqrt(ms + EPS) * g


def _per_seq(x, bsz, fn):
    rows, d = x.shape
    if bsz == 1:
        return fn(x)
    return fn(x.reshape(rows // bsz, bsz, d)).reshape(rows, d)


def _modulated_norm(x, g, sh, sc):
    return _per_seq(_rms(x, g), sh.shape[0], lambda y: y * (1.0 + sc) + sh)


def _gated_residual(x, y, g, gate):
    return x + _per_seq(_rms(y, g), gate.shape[0], lambda r: gate * r)


def _dot(a, b):
    return jnp.dot(a, b, preferred_element_type=F32)


def _relayout_scratch(rows, d):
    return [pltpu.VMEM((rows, LANES), F32)] * (d // LANES)


def _put_batch_major(dst_ref, j, val, tmp_ref):
    bsz, tt, _ = dst_ref.shape
    tmp_ref[...] = val
    for b in range(bsz):
        dst_ref[b, :, j * LANES:(j + 1) * LANES] = (
            tmp_ref[pl.ds(b, tt, stride=bsz), :].astype(dst_ref.dtype))


def _get_time_major(src_ref, tmp_refs):
    bsz, tt, _ = src_ref.shape
    for j, tmp_ref in enumerate(tmp_refs):
        for b in range(bsz):
            tmp_ref[pl.ds(b, tt, stride=bsz), :] = src_ref[b, :, j * LANES:(j + 1) * LANES].astype(F32)
    return jnp.concatenate([t[...] for t in tmp_refs], axis=1)


def _ada_kernel(c_ref, w_ref, b_ref, o_ref):
    c = c_ref[...]
    cond = (c * jax.nn.sigmoid(c)).astype(BF16)
    o_ref[...] = _dot(cond, w_ref[...].astype(BF16)) + b_ref[...]


def _ada_mod(c, ada_w, ada_b):
    depth, d, n = ada_w.shape
    bsz = c.shape[0]
    tn = n // 4
    return pl.pallas_call(
        _ada_kernel,
        grid=(depth, n // tn),
        in_specs=[
            pl.BlockSpec((bsz, d), lambda i, j: (0, 0)),
            pl.BlockSpec((None, d, tn), lambda i, j: (i, 0, j)),
            pl.BlockSpec((None, 1, tn), lambda i, j: (i, 0, j)),
        ],
        out_specs=pl.BlockSpec((None, bsz, tn), lambda i, j: (i, 0, j)),
        out_shape=jax.ShapeDtypeStruct((depth, bsz, n), F32),
        compiler_params=_params("parallel", "parallel"),
        name="ada_mod",
    )(c, ada_w, ada_b.reshape(depth, 1, n))


def _s5_disc_kernel(are_ref, aim_ref, ldt_ref, br_ref, bi_ref, lbr_ref, lbi_ref, bbr_ref, bbi_ref):
    lam_re = jnp.minimum(are_ref[...], S5_LAMBDA_RE_MAX)
    lam_im = aim_ref[...]
    dt = jnp.exp(ldt_ref[...])
    dre, dimg = lam_re * dt, lam_im * dt
    mag = jnp.exp(dre)
    lb_re, lb_im = mag * jnp.cos(dimg), mag * jnp.sin(dimg)
    den = lam_re * lam_re + lam_im * lam_im
    nr = lb_re - 1.0
    f_re = (nr * lam_re + lb_im * lam_im) / den
    f_im = (lb_im * lam_re - nr * lam_im) / den
    br, bi = br_ref[...], bi_ref[...]
    lbr_ref[...] = lb_re
    lbi_ref[...] = lb_im
    bbr_ref[...] = f_re * br - f_im * bi
    bbi_ref[...] = f_re * bi + f_im * br


def _s5_discretise(a_re, a_im, log_dt, b_re, b_im):
    n, g, p = a_re.shape
    cg = b_re.shape[-1]
    vec = pl.BlockSpec((None, g, 1, p), lambda i: (i, 0, 0, 0))
    mat = pl.BlockSpec((None, g, cg, p), lambda i: (i, 0, 0, 0))
    return pl.pallas_call(
        _s5_disc_kernel,
        grid=(n,),
        in_specs=[vec, vec, pl.BlockSpec((None, g, 1, 1), lambda i: (i, 0, 0, 0)), mat, mat],
        out_specs=[vec, vec, mat, mat],
        out_shape=[jax.ShapeDtypeStruct((n, g, 1, p), F32)] * 2
        + [jax.ShapeDtypeStruct((n, g, cg, p), F32)] * 2,
        compiler_params=_params("parallel"),
        name="s5_discretise",
    )(a_re.reshape(n, g, 1, p), a_im.reshape(n, g, 1, p), log_dt.reshape(n, g, 1, 1),
      jnp.swapaxes(b_re, -1, -2), jnp.swapaxes(b_im, -1, -2))


def _s5_pack_weights(lb_re, lb_im, bb_re, bb_im, c_re, c_im):
    g, cg, p = bb_re.shape
    gp = S5_PACK_GROUPS
    npack = g // gp
    eye = jnp.eye(gp, dtype=F32)

    def in_proj(bb):
        return jnp.einsum("kgcp,gh->kgchp", bb.reshape(npack, gp, cg, p), eye).reshape(
            npack, gp * cg, gp * p)

    def out_proj(c):
        return jnp.einsum("kgcp,gh->kgphc", c.reshape(npack, gp, cg, p), eye).reshape(
            npack, gp * p, gp * cg)

    wb = jnp.concatenate([in_proj(bb_re), in_proj(bb_im)], axis=-1).astype(BF16)
    return (wb, out_proj(c_re).astype(BF16), out_proj(c_im).astype(BF16),
            lb_re.reshape(npack, 1, gp * p), lb_im.reshape(npack, 1, gp * p))


def _s5_kernel(x_ref, g0_ref, sh_ref, sc_ref, gate_ref, g1_ref, wb_ref, lbr_ref, lbi_ref,
               wcr_ref, wci_ref, dsk_ref, wglu_ref, out_ref, h_ref, u_ref, bu_refs, y_ref,
               *relayout):
    if relayout:
        xtm_ref, *tmp_refs = relayout
        xtm_ref[...] = _get_time_major(x_ref, tmp_refs)
        x_ref = xtm_ref
    rows, d = x_ref.shape
    bsz = sh_ref.shape[0]
    npack = wb_ref.shape[0]
    ns = S5_PACK_STATES

    @pl.when(pl.program_id(0) == 0)
    def _():
        h_ref[...] = jnp.zeros_like(h_ref)

    u_ref[...] = _modulated_norm(x_ref[...], g0_ref[...], sh_ref[...], sc_ref[...])

    for k in range(npack):
        c0 = k * MXU_DIM
        bu_ref = bu_refs[k % 2]
        bu_ref[...] = _dot(u_ref[:, c0:c0 + MXU_DIM].astype(BF16), wb_ref[k])
        for s in range(ns // S5_SCAN_LANES):
            re = pl.ds(s * S5_SCAN_LANES, S5_SCAN_LANES)
            im = pl.ds(ns + s * S5_SCAN_LANES, S5_SCAN_LANES)
            lbr = jnp.broadcast_to(lbr_ref[k, :, re], (bsz, S5_SCAN_LANES))
            lbi = jnp.broadcast_to(lbi_ref[k, :, re], (bsz, S5_SCAN_LANES))
            hr, hi = h_ref[k, :, re], h_ref[k, :, im]
            for t in range(rows // bsz):
                frame = pl.ds(t * bsz, bsz)
                hr, hi = (lbr * hr - lbi * hi + bu_ref[frame, re],
                          lbr * hi + lbi * hr + bu_ref[frame, im])
                bu_ref[frame, re] = hr
                bu_ref[frame, im] = hi
            h_ref[k, :, re] = hr
            h_ref[k, :, im] = hi
        y_ref[:, c0:c0 + MXU_DIM] = (_dot(bu_ref[:, :ns].astype(BF16), wcr_ref[k])
                                     - _dot(bu_ref[:, ns:].astype(BF16), wci_ref[k]))

    y = y_ref[...] + dsk_ref[...] * u_ref[...]
    o = _dot(jax.nn.gelu(y).astype(BF16), wglu_ref[...])
    mix = o[:, :d] * jax.nn.sigmoid(o[:, d:])
    out_ref[...] = _gated_residual(x_ref[...], mix, g1_ref[...], gate_ref[...])


def _s5_layer(x, g0, sh, sc, gate, g1, packed, d_skip, w_glu):
    bsz, d = sh.shape
    n = x.size // d
    wb, wcr, wci, lbr, lbi = packed
    npack = wb.shape[0]
    rows = S5_FRAMES * bsz
    if x.ndim == 3:
        x_spec = pl.BlockSpec((bsz, S5_FRAMES, d), lambda t: (0, t, 0))
        relayout = [pltpu.VMEM((rows, d), F32)] + _relayout_scratch(rows, d)
    else:
        x_spec, relayout = _row_spec(rows, d), []
    return pl.pallas_call(
        _s5_kernel,
        grid=(n // rows,),
        in_specs=[x_spec, _const_spec((1, d)), _const_spec((bsz, d)),
                  _const_spec((bsz, d)), _const_spec((bsz, d)), _const_spec((1, d)),
                  _const_spec(wb.shape), _const_spec(lbr.shape), _const_spec(lbi.shape),
                  _const_spec(wcr.shape), _const_spec(wci.shape), _const_spec((1, d)),
                  _const_spec(w_glu.shape)],
        out_specs=_row_spec(rows, d),
        out_shape=jax.ShapeDtypeStruct((n, d), F32),
        scratch_shapes=[pltpu.VMEM((npack, bsz, 2 * S5_PACK_STATES), F32),
                        pltpu.VMEM((rows, d), F32),
                        [pltpu.VMEM((rows, 2 * S5_PACK_STATES), F32)] * 2,
                        pltpu.VMEM((rows, d), F32)] + relayout,
        compiler_params=_params("arbitrary"),
        name="s5_layer",
    )(x, g0, sh, sc, gate, g1, wb, lbr, lbi, wcr, wci, d_skip.reshape(1, d), w_glu)


def _rope_kernel(pos_ref, inv_ref, cos_ref, sin_ref):
    ang = pos_ref[...].astype(F32) * inv_ref[...]
    lane = lax.broadcasted_iota(jnp.int32, ang.shape, 1)
    first_half = lane % DA_HEAD_DIM < DA_HEAD_DIM // 2
    cos_ref[...] = jnp.cos(ang)
    sin_ref[...] = jnp.where(first_half, -1.0, 1.0) * jnp.sin(ang)


def _rope_tables(pos_rows):
    n = pos_rows.shape[0]
    half = DA_HEAD_DIM // 2
    inv = ROPE_THETA ** (-jnp.arange(half, dtype=F32) / half)
    inv = jnp.tile(inv, LANES // half).reshape(1, LANES)
    tm = min(n, ROW_TILE)
    return pl.pallas_call(
        _rope_kernel,
        grid=(n // tm,),
        in_specs=[_row_spec(tm, 1), pl.BlockSpec((1, LANES), lambda i: (0, 0))],
        out_specs=[_row_spec(tm, LANES)] * 2,
        out_shape=[jax.ShapeDtypeStruct((n, LANES), F32)] * 2,
        compiler_params=_params("parallel"),
        name="rope_tables",
    )(pos_rows, inv)


def _qkv_kernel(x_ref, g_ref, sh_ref, sc_ref, w_ref, cos_ref, sin_ref, q_ref, k_ref, v_ref,
                *tmp_refs):
    d = x_ref.shape[-1]
    h = _modulated_norm(x_ref[...], g_ref[...], sh_ref[...], sc_ref[...]).astype(BF16)
    cos, sin = cos_ref[...], sin_ref[...]
    lane = lax.broadcasted_iota(jnp.int32, cos.shape, 1)
    half = DA_HEAD_DIM // 2
    first_half = lane % DA_HEAD_DIM < half

    for src, dst, scale in ((0, q_ref, DA_HEAD_DIM ** -0.5), (d, k_ref, 1.0)):
        t = _dot(h, w_ref[:, src:src + d])
        for j in range(d // LANES):
            tj = t[:, j * LANES:(j + 1) * LANES]
            rot = jnp.where(first_half, pltpu.roll(tj, LANES - half, 1), pltpu.roll(tj, half, 1))
            _put_batch_major(dst, j, (tj * cos + rot * sin) * scale, tmp_refs[j])
    t = _dot(h, w_ref[:, 2 * d:])
    for j in range(d // LANES):
        _put_batch_major(v_ref, j, t[:, j * LANES:(j + 1) * LANES], tmp_refs[j])


def _qkv_proj(x, g, sh, sc, w_qkv, cos, sin):
    n, d = x.shape
    bsz = sh.shape[0]
    tm = min(n, ROW_TILE)
    batch_major = pl.BlockSpec((bsz, tm // bsz, d), lambda i: (0, i, 0))
    return pl.pallas_call(
        _qkv_kernel,
        grid=(n // tm,),
        in_specs=[_row_spec(tm, d), _const_spec((1, d)), _const_spec((bsz, d)),
                  _const_spec((bsz, d)), _const_spec(w_qkv.shape), _row_spec(tm, LANES),
                  _row_spec(tm, LANES)],
        out_specs=[batch_major] * 3,
        out_shape=[jax.ShapeDtypeStruct((bsz, n // bsz, d), BF16)] * 3,
        scratch_shapes=_relayout_scratch(tm, d),
        compiler_params=_params("parallel"),
        name="qkv_proj",
    )(x, g, sh, sc, w_qkv, cos, sin)


def _attn_kernel(q_ref, k_ref, v_ref, lq1_ref, lk1_ref, lq2_ref, lk2_ref, sg_ref, o_ref,
                 kt_ref, vext_ref, qs_ref, m_ref, acc_ref, s_refs, p_refs, a_refs, *,
                 lambda_init, tq):
    seq = q_ref.shape[0]
    vext_ref[:, :DA_V_DIM] = v_ref[...]
    col = lax.broadcasted_iota(jnp.int32, (seq, DA_V_DIM), 1)
    vext_ref[:, DA_V_DIM:] = jnp.where(col == 0, 1.0, 0.0).astype(BF16)
    tb = min(seq, ROW_TILE)
    for blk in range(seq // tb):
        kt_ref[:, blk * tb:(blk + 1) * tb] = (
            k_ref[blk * tb:(blk + 1) * tb, :].astype(F32).T.astype(BF16))

    lam = (jnp.exp(jnp.sum(lq1_ref[...] * lk1_ref[...], keepdims=True))
           - jnp.exp(jnp.sum(lq2_ref[...] * lk2_ref[...], keepdims=True)) + lambda_init)

    lane = lax.broadcasted_iota(jnp.int32, (tq, 2 * DA_HEAD_DIM), 1)
    first = lane < DA_HEAD_DIM
    rows = 2 * tq
    tk = min(seq, ATTN_TK)

    def row_ranges(blk):
        lo = blk[2] or 0
        return [(comp * tq + lo, (comp + 1) * tq) for comp in range(2)] if lo else [(0, rows)]

    def scores(t, blk):
        k0, kw, _ = blk
        kt = kt_ref[:, k0:k0 + kw]
        for r0, r1 in row_ranges(blk):
            s_refs[t % 2][r0:r1, :kw] = _dot(qs_ref[r0:r1, :], kt)

    def softmax(t, blk, first):
        _, kw, diag = blk
        for r0, r1 in row_ranges(blk):
            for i0 in range(r0, r1, ATTN_SM_ROWS):
                rs = pl.ds(i0, ATTN_SM_ROWS)
                s = s_refs[t % 2][rs, :kw]
                if diag is not None:
                    qrow = i0 % tq + lax.broadcasted_iota(jnp.int32, s.shape, 0)
                    key = diag + lax.broadcasted_iota(jnp.int32, s.shape, 1)
                    s = jnp.where(key < (qrow // CHUNK + 1) * CHUNK, s, NEG_BIG)
                parts = [s[:, j * LANES:(j + 1) * LANES] for j in range(kw // LANES)]
                m_new = jnp.max(functools.reduce(jnp.maximum, parts), axis=-1, keepdims=True)
                if first:
                    m_new = jnp.broadcast_to(m_new, (ATTN_SM_ROWS, LANES))
                else:
                    m_prev = m_ref[rs, :]
                    m_new = jnp.maximum(m_prev, m_new)
                    a_refs[t % 2][rs, :] = jnp.exp(m_prev - m_new)
                p_refs[t % 2][rs, :kw] = jnp.concatenate(
                    [jnp.exp(pj - m_new) for pj in parts], axis=1).astype(BF16)
                m_ref[rs, :] = m_new

    def values(t, blk, first):
        k0, kw, _ = blk
        vb = vext_ref[k0:k0 + kw, :]
        for r0, r1 in row_ranges(blk):
            pv = _dot(p_refs[t % 2][r0:r1, :kw], vb)
            if not first:
                alpha = a_refs[t % 2][r0:r1, :]
                pv = jnp.concatenate([alpha, alpha], axis=1) * acc_ref[r0:r1, :] + pv
            acc_ref[r0:r1, :] = pv

    for qi in range(seq // tq):
        q0 = qi * tq
        qb = q_ref[q0:q0 + tq, :]
        zero = jnp.zeros_like(qb)
        qs_ref[:tq, :] = jnp.where(first, qb, zero)
        qs_ref[tq:, :] = jnp.where(first, zero, qb)

        blocks = [(k0, min(tk, q0 - k0), None) for k0 in range(0, q0, tk)]
        blocks += [(q0 + off, ATTN_DIAG_KEYS, off) for off in range(0, tq, ATTN_DIAG_KEYS)]
        scores(0, blocks[0])
        for t, blk in enumerate(blocks):
            if t + 1 < len(blocks):
                scores(t + 1, blocks[t + 1])
            softmax(t, blk, t == 0)
            if t >= 1:
                values(t - 1, blocks[t - 1], t == 1)
        values(len(blocks) - 1, blocks[-1], False)

        for i in range(tq // ATTN_SM_ROWS):
            o = []
            for comp in range(2):
                r0 = comp * tq + i * ATTN_SM_ROWS
                acc = acc_ref[r0:r0 + ATTN_SM_ROWS, :]
                o.append(acc[:, :DA_V_DIM] / acc[:, DA_V_DIM:DA_V_DIM + 1])
            od = _rms(o[0] - lam * o[1], sg_ref[...]) * (1.0 - lambda_init)
            r0 = q0 + i * ATTN_SM_ROWS
            o_ref[r0:r0 + ATTN_SM_ROWS, :] = od.astype(BF16)


def _diff_attention(q, k, v, lq1, lk1, lq2, lk2, subln_g, lambda_init):
    bsz, seq, d = q.shape
    tq = min(seq, ATTN_TQ)
    rows, tk = 2 * tq, min(seq, ATTN_TK)
    head = pl.BlockSpec((None, seq, DA_V_DIM), lambda b, h: (b, 0, h))
    vec = _const_spec((1, DA_HEAD_DIM))
    return pl.pallas_call(
        functools.partial(_attn_kernel, lambda_init=lambda_init, tq=tq),
        grid=(bsz, d // DA_V_DIM),
        in_specs=[head, head, head, vec, vec, vec, vec, _const_spec((1, DA_V_DIM))],
        out_specs=head,
        out_shape=jax.ShapeDtypeStruct((bsz, seq, d), BF16),
        scratch_shapes=[pltpu.VMEM((2 * DA_HEAD_DIM, seq), BF16),
                        pltpu.VMEM((seq, 2 * DA_V_DIM), BF16),
                        pltpu.VMEM((rows, 2 * DA_HEAD_DIM), BF16),
                        pltpu.VMEM((rows, LANES), F32),
                        pltpu.VMEM((rows, 2 * DA_V_DIM), F32),
                        [pltpu.VMEM((rows, tk), F32)] * 2,
                        [pltpu.VMEM((rows, tk), BF16)] * 2,
                        [pltpu.VMEM((rows, LANES), F32)] * 2],
        compiler_params=_params("parallel", "parallel"),
        name="diff_attention",
    )(q, k, v, lq1.reshape(1, -1), lk1.reshape(1, -1), lq2.reshape(1, -1), lk2.reshape(1, -1),
      subln_g.reshape(1, -1))


def _oproj_kernel(o_ref, x_ref, w_ref, g_ref, gate_ref, out_ref, *tmp_refs):
    o = _get_time_major(o_ref, tmp_refs).astype(BF16)
    out_ref[...] = _gated_residual(x_ref[...], _dot(o, w_ref[...]), g_ref[...], gate_ref[...])


def _out_proj(o, x, w_o, g, gate):
    n, d = x.shape
    bsz = gate.shape[0]
    tm = min(n, ROW_TILE)
    return pl.pallas_call(
        _oproj_kernel,
        grid=(n // tm,),
        in_specs=[pl.BlockSpec((bsz, tm // bsz, d), lambda i: (0, i, 0)), _row_spec(tm, d),
                  _const_spec(w_o.shape), _const_spec((1, d)), _const_spec((bsz, d))],
        out_specs=_row_spec(tm, d),
        out_shape=jax.ShapeDtypeStruct(x.shape, F32),
        scratch_shapes=_relayout_scratch(tm, d),
        compiler_params=_params("parallel"),
        name="attn_out_proj",
    )(o, x, w_o, g, gate)


def _ffn_kernel(x_ref, xprev_ref, g2_ref, sh_ref, sc_ref, gate_ref, g3_ref, win_ref, cw_ref, cb_ref,
                wout_ref, out_ref, h_ref, acc_ref, u_refs, g_ref, *tmp_refs):
    tm, d = x_ref.shape
    f = wout_ref.shape[0]
    bsz = sh_ref.shape[0]
    halo = xprev_ref.shape[0]
    tf = FFN_TF
    n_tiles = f // tf
    g2, sh, sc = g2_ref[...], sh_ref[...], sc_ref[...]
    h_ref[:halo, :] = _modulated_norm(xprev_ref[...], g2, sh, sc).astype(BF16)
    h_ref[halo:, :] = _modulated_norm(x_ref[...], g2, sh, sc).astype(BF16)
    keep_halo = jnp.where(pl.program_id(0) > 0, 1.0, 0.0)
    acc_ref[...] = jnp.zeros_like(acc_ref)

    def up(i, slot):
        c0 = i * tf
        hb = h_ref[...]
        for half, col in enumerate((pl.ds(c0, tf), pl.ds(f + c0, tf))):
            u_refs[slot][:, half * tf:(half + 1) * tf] = _dot(hb, win_ref[:, col])
        u_refs[slot][:halo, :] = u_refs[slot][:halo, :] * keep_halo

    def gate(i, slot):
        c0 = i * tf
        halves = []
        for half, col in enumerate((pl.ds(c0, tf), pl.ds(f + c0, tf))):
            w = cw_ref[:, col].astype(BF16)
            out = cb_ref[:, col].astype(BF16)
            for j in range(CONV_WIDTH):
                u = u_refs[slot][pl.ds(j * bsz, tm), half * tf:(half + 1) * tf]
                out = out + w[j:j + 1] * u.astype(BF16)
            halves.append(out)
        g_ref[:, slot * tf:(slot + 1) * tf] = jax.nn.gelu(halves[0]) * halves[1]

    def down(i, ntile):
        c0 = i * tf
        acc_ref[...] += _dot(g_ref[:, :ntile * tf], wout_ref[pl.ds(c0, ntile * tf), :])

    assert n_tiles % 2 == 1
    up(0, 0)
    for i in range(0, n_tiles - 1, 2):
        up(i + 1, 1)
        gate(i, 0)
        up(i + 2, 0)
        gate(i + 1, 1)
        down(i, 2)
    gate(n_tiles - 1, 0)
    down(n_tiles - 1, 1)
    res = _gated_residual(x_ref[...], acc_ref[...], g3_ref[...], gate_ref[...])
    if tmp_refs:
        for j, tmp_ref in enumerate(tmp_refs):
            _put_batch_major(out_ref, j, res[:, j * LANES:(j + 1) * LANES], tmp_ref)
    else:
        out_ref[...] = res


def _conv_ffn(x, g2, sh, sc, gate, g3, w_in, conv_w, conv_b, w_out, batch_major_out=False):
    n, d = x.shape
    f = w_out.shape[0]
    bsz = sh.shape[0]
    tm = min(n, ROW_TILE)
    halo = (CONV_WIDTH - 1) * bsz
    prev = pl.BlockSpec((halo, d), lambda i: (jnp.maximum(i * (tm // halo) - 1, 0), 0))
    vec = _const_spec((bsz, d))
    if batch_major_out:
        out_spec = pl.BlockSpec((bsz, tm // bsz, d), lambda i: (0, i, 0))
        out_shape, relayout = (bsz, n // bsz, d), _relayout_scratch(tm, d)
    else:
        out_spec, out_shape, relayout = _row_spec(tm, d), (n, d), []
    return pl.pallas_call(
        _ffn_kernel,
        grid=(n // tm,),
        in_specs=[_row_spec(tm, d), prev, _const_spec((1, d)), vec, vec, vec, _const_spec((1, d)),
                  _const_spec(w_in.shape), _const_spec(conv_w.shape), _const_spec((1, 2 * f)),
                  _const_spec(w_out.shape)],
        out_specs=out_spec,
        out_shape=jax.ShapeDtypeStruct(out_shape, F32),
        scratch_shapes=[pltpu.VMEM((tm + halo, d), BF16),
                        pltpu.VMEM((tm, d), F32),
                        [pltpu.VMEM((tm + halo, 2 * FFN_TF), F32)] * 2,
                        pltpu.VMEM((tm, 2 * FFN_TF), BF16)] + relayout,
        compiler_params=_params("parallel"),
        name="conv_ffn",
    )(x, x, g2, sh, sc, gate, g3, w_in, conv_w, conv_b.reshape(1, 2 * f), w_out)


def kernel(x, c, positions, ada_w, ada_b, norm_g, s5_a_re, s5_a_im, s5_log_dt, s5_b_re, s5_b_im,
           s5_c_re, s5_c_im, s5_d, s5_w_glu, da_w_qkv, da_w_o, da_lq1, da_lk1, da_lq2, da_lk2,
           da_subln_g, ffn_w_in, ffn_conv_w, ffn_conv_b, ffn_w_out):
    depth = ada_w.shape[0]
    bsz, seq, d = x.shape
    n = seq * bsz
    mod = _ada_mod(c, ada_w, ada_b).reshape(depth, bsz, 6, d)
    lb_re, lb_im, bb_re, bb_im = _s5_discretise(s5_a_re, s5_a_im, s5_log_dt, s5_b_re, s5_b_im)
    cos, sin = _rope_tables(positions.T.reshape(n, 1))
    for i in range(depth):
        sh_t, sc_t, g_t, sh_c, sc_c, g_c = (mod[i, :, m] for m in range(6))
        gains = norm_g[i].reshape(4, 1, d)
        j = i // 2
        if i % 2 == 0:
            packed = _s5_pack_weights(lb_re[j], lb_im[j], bb_re[j], bb_im[j], s5_c_re[j], s5_c_im[j])
            x = _s5_layer(x, gains[0], sh_t, sc_t, g_t, gains[1], packed, s5_d[j],
                          s5_w_glu[j].astype(BF16))
        else:
            lambda_init = 0.8 - 0.6 * math.exp(-0.3 * i)
            q, k, v = _qkv_proj(x, gains[0], sh_t, sc_t, da_w_qkv[j].astype(BF16), cos, sin)
            o = _diff_attention(q, k, v, da_lq1[j], da_lk1[j], da_lq2[j], da_lk2[j],
                                da_subln_g[j], lambda_init)
            x = _out_proj(o, x, da_w_o[j].astype(BF16), gains[1], g_t)
        x = _conv_ffn(x, gains[2], sh_c, sc_c, g_c, gains[3], ffn_w_in[i].astype(BF16),
                      ffn_conv_w[i], ffn_conv_b[i], ffn_w_out[i].astype(BF16),
                      batch_major_out=(i == depth - 1))
    return x
```

```python
import functools
import math

import jax
import jax.numpy as jnp
from jax import lax
from jax.experimental import pallas as pl
from jax.experimental.pallas import tpu as pltpu

F32 = jnp.float32
BF16 = jnp.bfloat16

EPS = 1e-6
CHUNK = 64
S5_GROUP = 16
S5_STATE = 64
S5_LAMBDA_RE_MAX = -1e-4
DA_HEADS = 8
DA_HEAD_DIM = 64
DA_V_DIM = 2 * DA_HEAD_DIM
ROPE_THETA = 10000.0
CONV_WIDTH = 3

LANES = 128
MXU_DIM = 256
VMEM_LIMIT_BYTES = 56 * 1024 * 1024

S5_PACK_GROUPS = MXU_DIM // S5_GROUP
S5_PACK_STATES = S5_PACK_GROUPS * S5_STATE
S5_SCAN_LANES = 4 * LANES
S5_FRAMES = 128

ROW_TILE = 1024
ATTN_TQ = 512
ATTN_TK = 1024
ATTN_DIAG_KEYS = 256
ATTN_SM_ROWS = 256
FFN_TF = 256
NEG_BIG = -1e30


def _params(*sem):
    return pltpu.CompilerParams(dimension_semantics=sem, vmem_limit_bytes=VMEM_LIMIT_BYTES)


def _const_spec(shape):
    nd = len(shape)
    return pl.BlockSpec(shape, lambda *_: (0,) * nd, pipeline_mode=pl.Buffered(1))


def _layer_spec(stack, layer):
    nd = stack.ndim
    return pl.BlockSpec((None,) + stack.shape[1:], lambda *_: (layer,) + (0,) * (nd - 1),
                        pipeline_mode=pl.Buffered(1))


def _row_spec(tm, width):
    return pl.BlockSpec((tm, width), lambda i: (i, 0))


def _rms(x, g):
    ms = jnp.mean(x * x, axis=-1, keepdims=True)
    return x * lax.rsqrt(ms + EPS) * g


def _per_seq(x, bsz, fn):
    rows, d = x.shape
    if bsz == 1:
        return fn(x)
    return fn(x.reshape(rows // bsz, bsz, d)).reshape(rows, d)


def _modulated_norm(x, g, sh, sc):
    return _per_seq(_rms(x, g), sh.shape[0], lambda y: y * (1.0 + sc) + sh)


def _gated_residual(x, y, g, gate):
    return x + _per_seq(_rms(y, g), gate.shape[0], lambda r: gate * r)


def _dot(a, b):
    return jnp.dot(a, b, preferred_element_type=F32)


def _relayout_scratch(rows, d):
    return [pltpu.VMEM((rows, LANES), F32)] * (d // LANES)


def _put_batch_major(dst_ref, j, val, tmp_ref):
    bsz, tt, _ = dst_ref.shape
    tmp_ref[...] = val
    for b in range(bsz):
        dst_ref[b, :, j * LANES:(j + 1) * LANES] = (
            tmp_ref[pl.ds(b, tt, stride=bsz), :].astype(dst_ref.dtype))


def _get_time_major(src_ref, tmp_refs):
    bsz, tt, _ = src_ref.shape
    for j, tmp_ref in enumerate(tmp_refs):
        for b in range(bsz):
            tmp_ref[pl.ds(b, tt, stride=bsz), :] = src_ref[b, :, j * LANES:(j + 1) * LANES].astype(F32)
    return jnp.concatenate([t[...] for t in tmp_refs], axis=1)


def _ada_kernel(c_ref, w_ref, b_ref, o_ref):
    c = c_ref[...]
    cond = (c * jax.nn.sigmoid(c)).astype(BF16)
    o_ref[...] = _dot(cond, w_ref[...].astype(BF16)) + b_ref[...]


def _ada_mod(c, ada_w, ada_b):
    depth, d, n = ada_w.shape
    bsz = c.shape[0]
    tn = n // 4
    return pl.pallas_call(
        _ada_kernel,
        grid=(depth, n // tn),
        in_specs=[
            pl.BlockSpec((bsz, d), lambda i, j: (0, 0)),
            pl.BlockSpec((None, d, tn), lambda i, j: (i, 0, j)),
            pl.BlockSpec((None, 1, tn), lambda i, j: (i, 0, j)),
        ],
        out_specs=pl.BlockSpec((None, bsz, tn), lambda i, j: (i, 0, j)),
        out_shape=jax.ShapeDtypeStruct((depth, bsz, n), F32),
        compiler_params=_params("parallel", "parallel"),
        name="ada_mod",
    )(c, ada_w, ada_b.reshape(depth, 1, n))


def _s5_disc_kernel(are_ref, aim_ref, ldt_ref, br_ref, bi_ref, lbr_ref, lbi_ref, bbr_ref, bbi_ref):
    lam_re = jnp.minimum(are_ref[...], S5_LAMBDA_RE_MAX)
    lam_im = aim_ref[...]
    dt = jnp.exp(ldt_ref[...])
    dre, dimg = lam_re * dt, lam_im * dt
    mag = jnp.exp(dre)
    lb_re, lb_im = mag * jnp.cos(dimg), mag * jnp.sin(dimg)
    den = lam_re * lam_re + lam_im * lam_im
    nr = lb_re - 1.0
    f_re = (nr * lam_re + lb_im * lam_im) / den
    f_im = (lb_im * lam_re - nr * lam_im) / den
    br, bi = br_ref[...], bi_ref[...]
    lbr_ref[...] = lb_re
    lbi_ref[...] = lb_im
    bbr_ref[...] = f_re * br - f_im * bi
    bbi_ref[...] = f_re * bi + f_im * br


def _s5_discretise(a_re, a_im, log_dt, b_re, b_im):
    n, g, p = a_re.shape
    cg = b_re.shape[-1]
    vec = pl.BlockSpec((None, g, 1, p), lambda i: (i, 0, 0, 0))
    mat = pl.BlockSpec((None, g, cg, p), lambda i: (i, 0, 0, 0))
    return pl.pallas_call(
        _s5_disc_kernel,
        grid=(n,),
        in_specs=[vec, vec, pl.BlockSpec((None, g, 1, 1), lambda i: (i, 0, 0, 0)), mat, mat],
        out_specs=[vec, vec, mat, mat],
        out_shape=[jax.ShapeDtypeStruct((n, g, 1, p), F32)] * 2
        + [jax.ShapeDtypeStruct((n, g, cg, p), F32)] * 2,
        compiler_params=_params("parallel"),
        name="s5_discretise",
    )(a_re.reshape(n, g, 1, p), a_im.reshape(n, g, 1, p), log_dt.reshape(n, g, 1, 1),
      jnp.swapaxes(b_re, -1, -2), jnp.swapaxes(b_im, -1, -2))


def _s5_pack_weights(lb_re, lb_im, bb_re, bb_im, c_re, c_im):
    g, cg, p = bb_re.shape
    gp = S5_PACK_GROUPS
    npack = g // gp
    eye = jnp.eye(gp, dtype=F32)

    def in_proj(bb):
        return jnp.einsum("kgcp,gh->kgchp", bb.reshape(npack, gp, cg, p), eye).reshape(
            npack, gp * cg, gp * p)

    def out_proj(c):
        return jnp.einsum("kgcp,gh->kgphc", c.reshape(npack, gp, cg, p), eye).reshape(
            npack, gp * p, gp * cg)

    wb = jnp.concatenate([in_proj(bb_re), in_proj(bb_im)], axis=-1).astype(BF16)
    return (wb, out_proj(c_re).astype(BF16), out_proj(c_im).astype(BF16),
            lb_re.reshape(npack, 1, gp * p), lb_im.reshape(npack, 1, gp * p))


def _s5_kernel(x_ref, g0_ref, sh_ref, sc_ref, gate_ref, g1_ref, wb_ref, lbr_ref, lbi_ref,
               wcr_ref, wci_ref, dsk_ref, wglu_ref, out_ref, h_ref, u_ref, bu_refs, y_ref,
               *relayout):
    if relayout:
        xtm_ref, *tmp_refs = relayout
        xtm_ref[...] = _get_time_major(x_ref, tmp_refs)
        x_ref = xtm_ref
    rows, d = x_ref.shape
    bsz = sh_ref.shape[0]
    npack = wb_ref.shape[0]
    ns = S5_PACK_STATES

    @pl.when(pl.program_id(0) == 0)
    def _():
        h_ref[...] = jnp.zeros_like(h_ref)

    u_ref[...] = _modulated_norm(x_ref[...], g0_ref[...], sh_ref[...], sc_ref[...])

    for k in range(npack):
        c0 = k * MXU_DIM
        bu_ref = bu_refs[k % 2]
        bu_ref[...] = _dot(u_ref[:, c0:c0 + MXU_DIM].astype(BF16), wb_ref[k])
        for s in range(ns // S5_SCAN_LANES):
            re = pl.ds(s * S5_SCAN_LANES, S5_SCAN_LANES)
            im = pl.ds(ns + s * S5_SCAN_LANES, S5_SCAN_LANES)
            lbr = jnp.broadcast_to(lbr_ref[k, :, re], (bsz, S5_SCAN_LANES))
            lbi = jnp.broadcast_to(lbi_ref[k, :, re], (bsz, S5_SCAN_LANES))
            hr, hi = h_ref[k, :, re], h_ref[k, :, im]
            for t in range(rows // bsz):
                frame = pl.ds(t * bsz, bsz)
                hr, hi = (lbr * hr - lbi * hi + bu_ref[frame, re],
                          lbr * hi + lbi * hr + bu_ref[frame, im])
                bu_ref[frame, re] = hr
                bu_ref[frame, im] = hi
            h_ref[k, :, re] = hr
            h_ref[k, :, im] = hi
        y_ref[:, c0:c0 + MXU_DIM] = (_dot(bu_ref[:, :ns].astype(BF16), wcr_ref[k])
                                     - _dot(bu_ref[:, ns:].astype(BF16), wci_ref[k]))

    y = y_ref[...] + dsk_ref[...] * u_ref[...]
    o = _dot(jax.nn.gelu(y).astype(BF16), wglu_ref[...])
    mix = o[:, :d] * jax.nn.sigmoid(o[:, d:])
    out_ref[...] = _gated_residual(x_ref[...], mix, g1_ref[...], gate_ref[...])


def _s5_layer(x, g0, sh, sc, gate, g1, packed, d_skip, w_glu, layer):
    bsz, d = sh.shape
    n = x.size // d
    wb, wcr, wci, lbr, lbi = packed
    npack = wb.shape[0]
    rows = S5_FRAMES * bsz
    if x.ndim == 3:
        x_spec = pl.BlockSpec((bsz, S5_FRAMES, d), lambda t: (0, t, 0))
        relayout = [pltpu.VMEM((rows, d), F32)] + _relayout_scratch(rows, d)
    else:
        x_spec, relayout = _row_spec(rows, d), []
    return pl.pallas_call(
        _s5_kernel,
        grid=(n // rows,),
        in_specs=[x_spec, _const_spec((1, d)), _const_spec((bsz, d)),
                  _const_spec((bsz, d)), _const_spec((bsz, d)), _const_spec((1, d)),
                  _const_spec(wb.shape), _const_spec(lbr.shape), _const_spec(lbi.shape),
                  _const_spec(wcr.shape), _const_spec(wci.shape), _const_spec((1, d)),
                  _layer_spec(w_glu, layer)],
        out_specs=_row_spec(rows, d),
        out_shape=jax.ShapeDtypeStruct((n, d), F32),
        scratch_shapes=[pltpu.VMEM((npack, bsz, 2 * S5_PACK_STATES), F32),
                        pltpu.VMEM((rows, d), F32),
                        [pltpu.VMEM((rows, 2 * S5_PACK_STATES), F32)] * 2,
                        pltpu.VMEM((rows, d), F32)] + relayout,
        compiler_params=_params("arbitrary"),
        name="s5_layer",
    )(x, g0, sh, sc, gate, g1, wb, lbr, lbi, wcr, wci, d_skip.reshape(1, d), w_glu)


def _rope_kernel(pos_ref, inv_ref, cos_ref, sin_ref):
    ang = pos_ref[...].astype(F32) * inv_ref[...]
    lane = lax.broadcasted_iota(jnp.int32, ang.shape, 1)
    first_half = lane % DA_HEAD_DIM < DA_HEAD_DIM // 2
    cos_ref[...] = jnp.cos(ang)
    sin_ref[...] = jnp.where(first_half, -1.0, 1.0) * jnp.sin(ang)


def _rope_tables(pos_rows):
    n = pos_rows.shape[0]
    half = DA_HEAD_DIM // 2
    inv = ROPE_THETA ** (-jnp.arange(half, dtype=F32) / half)
    inv = jnp.tile(inv, LANES // half).reshape(1, LANES)
    tm = min(n, ROW_TILE)
    return pl.pallas_call(
        _rope_kernel,
        grid=(n // tm,),
        in_specs=[_row_spec(tm, 1), pl.BlockSpec((1, LANES), lambda i: (0, 0))],
        out_specs=[_row_spec(tm, LANES)] * 2,
        out_shape=[jax.ShapeDtypeStruct((n, LANES), F32)] * 2,
        compiler_params=_params("parallel"),
        name="rope_tables",
    )(pos_rows, inv)


def _qkv_kernel(x_ref, g_ref, sh_ref, sc_ref, w_ref, cos_ref, sin_ref, q_ref, k_ref, v_ref,
                *tmp_refs):
    d = x_ref.shape[-1]
    h = _modulated_norm(x_ref[...], g_ref[...], sh_ref[...], sc_ref[...]).astype(BF16)
    cos, sin = cos_ref[...], sin_ref[...]
    lane = lax.broadcasted_iota(jnp.int32, cos.shape, 1)
    half = DA_HEAD_DIM // 2
    first_half = lane % DA_HEAD_DIM < half

    for src, dst, scale in ((0, q_ref, DA_HEAD_DIM ** -0.5), (d, k_ref, 1.0)):
        t = _dot(h, w_ref[:, src:src + d])
        for j in range(d // LANES):
            tj = t[:, j * LANES:(j + 1) * LANES]
            rot = jnp.where(first_half, pltpu.roll(tj, LANES - half, 1), pltpu.roll(tj, half, 1))
            _put_batch_major(dst, j, (tj * cos + rot * sin) * scale, tmp_refs[j])
    t = _dot(h, w_ref[:, 2 * d:])
    for j in range(d // LANES):
        _put_batch_major(v_ref, j, t[:, j * LANES:(j + 1) * LANES], tmp_refs[j])


def _qkv_proj(x, g, sh, sc, w_qkv, layer, cos, sin):
    n, d = x.shape
    bsz = sh.shape[0]
    tm = min(n, ROW_TILE)
    batch_major = pl.BlockSpec((bsz, tm // bsz, d), lambda i: (0, i, 0))
    return pl.pallas_call(
        _qkv_kernel,
        grid=(n // tm,),
        in_specs=[_row_spec(tm, d), _const_spec((1, d)), _const_spec((bsz, d)),
                  _const_spec((bsz, d)), _layer_spec(w_qkv, layer), _row_spec(tm, LANES),
                  _row_spec(tm, LANES)],
        out_specs=[batch_major] * 3,
        out_shape=[jax.ShapeDtypeStruct((bsz, n // bsz, d), BF16)] * 3,
        scratch_shapes=_relayout_scratch(tm, d),
        compiler_params=_params("parallel"),
        name="qkv_proj",
    )(x, g, sh, sc, w_qkv, cos, sin)


def _attn_kernel(q_ref, k_ref, v_ref, lq1_ref, lk1_ref, lq2_ref, lk2_ref, sg_ref, o_ref,
                 kt_ref, vext_ref, qs_ref, m_ref, acc_ref, s_refs, p_refs, a_refs, *,
                 lambda_init, tq):
    seq = q_ref.shape[0]
    vext_ref[:, :DA_V_DIM] = v_ref[...]
    col = lax.broadcasted_iota(jnp.int32, (seq, DA_V_DIM), 1)
    vext_ref[:, DA_V_DIM:] = jnp.where(col == 0, 1.0, 0.0).astype(BF16)
    tb = min(seq, ROW_TILE)
    for blk in range(seq // tb):
        kt_ref[:, blk * tb:(blk + 1) * tb] = (
            k_ref[blk * tb:(blk + 1) * tb, :].astype(F32).T.astype(BF16))

    lam = (jnp.exp(jnp.sum(lq1_ref[...] * lk1_ref[...], keepdims=True))
           - jnp.exp(jnp.sum(lq2_ref[...] * lk2_ref[...], keepdims=True)) + lambda_init)

    lane = lax.broadcasted_iota(jnp.int32, (tq, 2 * DA_HEAD_DIM), 1)
    first = lane < DA_HEAD_DIM
    rows = 2 * tq
    tk = min(seq, ATTN_TK)

    def row_ranges(blk):
        lo = blk[2] or 0
        return [(comp * tq + lo, (comp + 1) * tq) for comp in range(2)] if lo else [(0, rows)]

    def scores(t, blk):
        k0, kw, _ = blk
        kt = kt_ref[:, k0:k0 + kw]
        for r0, r1 in row_ranges(blk):
            s_refs[t % 2][r0:r1, :kw] = _dot(qs_ref[r0:r1, :], kt)

    def softmax(t, blk, first):
        _, kw, diag = blk
        for r0, r1 in row_ranges(blk):
            for i0 in range(r0, r1, ATTN_SM_ROWS):
                rs = pl.ds(i0, ATTN_SM_ROWS)
                s = s_refs[t % 2][rs, :kw]
                if diag is not None:
                    qrow = i0 % tq + lax.broadcasted_iota(jnp.int32, s.shape, 0)
                    key = diag + lax.broadcasted_iota(jnp.int32, s.shape, 1)
                    s = jnp.where(key < (qrow // CHUNK + 1) * CHUNK, s, NEG_BIG)
                parts = [s[:, j * LANES:(j + 1) * LANES] for j in range(kw // LANES)]
                m_new = jnp.max(functools.reduce(jnp.maximum, parts), axis=-1, keepdims=True)
                if first:
                    m_new = jnp.broadcast_to(m_new, (ATTN_SM_ROWS, LANES))
                else:
                    m_prev = m_ref[rs, :]
                    m_new = jnp.maximum(m_prev, m_new)
                    a_refs[t % 2][rs, :] = jnp.exp(m_prev - m_new)
                p_refs[t % 2][rs, :kw] = jnp.concatenate(
                    [jnp.exp(pj - m_new) for pj in parts], axis=1).astype(BF16)
                m_ref[rs, :] = m_new

    def values(t, blk, first):
        k0, kw, _ = blk
        vb = vext_ref[k0:k0 + kw, :]
        for r0, r1 in row_ranges(blk):
            pv = _dot(p_refs[t % 2][r0:r1, :kw], vb)
            if not first:
                alpha = a_refs[t % 2][r0:r1, :]
                pv = jnp.concatenate([alpha, alpha], axis=1) * acc_ref[r0:r1, :] + pv
            acc_ref[r0:r1, :] = pv

    for qi in range(seq // tq):
        q0 = qi * tq
        qb = q_ref[q0:q0 + tq, :]
        zero = jnp.zeros_like(qb)
        qs_ref[:tq, :] = jnp.where(first, qb, zero)
        qs_ref[tq:, :] = jnp.where(first, zero, qb)

        blocks = [(k0, min(tk, q0 - k0), None) for k0 in range(0, q0, tk)]
        blocks += [(q0 + off, ATTN_DIAG_KEYS, off) for off in range(0, tq, ATTN_DIAG_KEYS)]
        scores(0, blocks[0])
        for t, blk in enumerate(blocks):
            if t + 1 < len(blocks):
                scores(t + 1, blocks[t + 1])
            softmax(t, blk, t == 0)
            if t >= 1:
                values(t - 1, blocks[t - 1], t == 1)
        values(len(blocks) - 1, blocks[-1], False)

        for i in range(tq // ATTN_SM_ROWS):
            o = []
            for comp in range(2):
                r0 = comp * tq + i * ATTN_SM_ROWS
                acc = acc_ref[r0:r0 + ATTN_SM_ROWS, :]
                o.append(acc[:, :DA_V_DIM] / acc[:, DA_V_DIM:DA_V_DIM + 1])
            od = _rms(o[0] - lam * o[1], sg_ref[...]) * (1.0 - lambda_init)
            r0 = q0 + i * ATTN_SM_ROWS
            o_ref[r0:r0 + ATTN_SM_ROWS, :] = od.astype(BF16)


def _diff_attention(q, k, v, lq1, lk1, lq2, lk2, subln_g, lambda_init):
    bsz, seq, d = q.shape
    tq = min(seq, ATTN_TQ)
    rows, tk = 2 * tq, min(seq, ATTN_TK)
    head = pl.BlockSpec((None, seq, DA_V_DIM), lambda b, h: (b, 0, h))
    vec = _const_spec((1, DA_HEAD_DIM))
    return pl.pallas_call(
        functools.partial(_attn_kernel, lambda_init=lambda_init, tq=tq),
        grid=(bsz, d // DA_V_DIM),
        in_specs=[head, head, head, vec, vec, vec, vec, _const_spec((1, DA_V_DIM))],
        out_specs=head,
        out_shape=jax.ShapeDtypeStruct((bsz, seq, d), BF16),
        scratch_shapes=[pltpu.VMEM((2 * DA_HEAD_DIM, seq), BF16),
                        pltpu.VMEM((seq, 2 * DA_V_DIM), BF16),
                        pltpu.VMEM((rows, 2 * DA_HEAD_DIM), BF16),
                        pltpu.VMEM((rows, LANES), F32),
                        pltpu.VMEM((rows, 2 * DA_V_DIM), F32),
                        [pltpu.VMEM((rows, tk), F32)] * 2,
                        [pltpu.VMEM((rows, tk), BF16)] * 2,
                        [pltpu.VMEM((rows, LANES), F32)] * 2],
        compiler_params=_params("parallel", "parallel"),
        name="diff_attention",
    )(q, k, v, lq1.reshape(1, -1), lk1.reshape(1, -1), lq2.reshape(1, -1), lk2.reshape(1, -1),
      subln_g.reshape(1, -1))


def _oproj_kernel(o_ref, x_ref, w_ref, g_ref, gate_ref, out_ref, *tmp_refs):
    o = _get_time_major(o_ref, tmp_refs).astype(BF16)
    out_ref[...] = _gated_residual(x_ref[...], _dot(o, w_ref[...]), g_ref[...], gate_ref[...])


def _out_proj(o, x, w_o, layer, g, gate):
    n, d = x.shape
    bsz = gate.shape[0]
    tm = min(n, ROW_TILE)
    return pl.pallas_call(
        _oproj_kernel,
        grid=(n // tm,),
        in_specs=[pl.BlockSpec((bsz, tm // bsz, d), lambda i: (0, i, 0)), _row_spec(tm, d),
                  _layer_spec(w_o, layer), _const_spec((1, d)), _const_spec((bsz, d))],
        out_specs=_row_spec(tm, d),
        out_shape=jax.ShapeDtypeStruct(x.shape, F32),
        scratch_shapes=_relayout_scratch(tm, d),
        compiler_params=_params("parallel"),
        name="attn_out_proj",
    )(o, x, w_o, g, gate)


def _ffn_kernel(x_ref, xprev_ref, g2_ref, sh_ref, sc_ref, gate_ref, g3_ref, win_ref, cw_ref, cb_ref,
                wout_ref, out_ref, h_ref, acc_ref, u_refs, g_ref, *tmp_refs):
    tm, d = x_ref.shape
    f = wout_ref.shape[0]
    bsz = sh_ref.shape[0]
    halo = xprev_ref.shape[0]
    tf = FFN_TF
    n_tiles = f // tf
    g2, sh, sc = g2_ref[...], sh_ref[...], sc_ref[...]
    h_ref[:halo, :] = _modulated_norm(xprev_ref[...], g2, sh, sc).astype(BF16)
    h_ref[halo:, :] = _modulated_norm(x_ref[...], g2, sh, sc).astype(BF16)
    keep_halo = jnp.where(pl.program_id(0) > 0, 1.0, 0.0)
    acc_ref[...] = jnp.zeros_like(acc_ref)

    def up(i, slot):
        c0 = i * tf
        hb = h_ref[...]
        for half, col in enumerate((pl.ds(c0, tf), pl.ds(f + c0, tf))):
            u_refs[slot][:, half * tf:(half + 1) * tf] = _dot(hb, win_ref[:, col])
        u_refs[slot][:halo, :] = u_refs[slot][:halo, :] * keep_halo

    def gate(i, slot):
        c0 = i * tf
        halves = []
        for half, col in enumerate((pl.ds(c0, tf), pl.ds(f + c0, tf))):
            w = cw_ref[:, col].astype(BF16)
            out = cb_ref[:, col].astype(BF16)
            for j in range(CONV_WIDTH):
                u = u_refs[slot][pl.ds(j * bsz, tm), half * tf:(half + 1) * tf]
                out = out + w[j:j + 1] * u.astype(BF16)
            halves.append(out)
        g_ref[:, slot * tf:(slot + 1) * tf] = jax.nn.gelu(halves[0]) * halves[1]

    def down(i, ntile):
        c0 = i * tf
        acc_ref[...] += _dot(g_ref[:, :ntile * tf], wout_ref[pl.ds(c0, ntile * tf), :])

    assert n_tiles % 2 == 1
    up(0, 0)
    for i in range(0, n_tiles - 1, 2):
        up(i + 1, 1)
        gate(i, 0)
        up(i + 2, 0)
        gate(i + 1, 1)
        down(i, 2)
    gate(n_tiles - 1, 0)
    down(n_tiles - 1, 1)
    res = _gated_residual(x_ref[...], acc_ref[...], g3_ref[...], gate_ref[...])
    if tmp_refs:
        for j, tmp_ref in enumerate(tmp_refs):
            _put_batch_major(out_ref, j, res[:, j * LANES:(j + 1) * LANES], tmp_ref)
    else:
        out_ref[...] = res


def _conv_ffn(x, g2, sh, sc, gate, g3, w_in, conv_w, conv_b, w_out, layer, batch_major_out=False):
    n, d = x.shape
    f = w_out.shape[1]
    bsz = sh.shape[0]
    tm = min(n, ROW_TILE)
    halo = (CONV_WIDTH - 1) * bsz
    prev = pl.BlockSpec((halo, d), lambda i: (jnp.maximum(i * (tm // halo) - 1, 0), 0))
    vec = _const_spec((bsz, d))
    if batch_major_out:
        out_spec = pl.BlockSpec((bsz, tm // bsz, d), lambda i: (0, i, 0))
        out_shape, relayout = (bsz, n // bsz, d), _relayout_scratch(tm, d)
    else:
        out_spec, out_shape, relayout = _row_spec(tm, d), (n, d), []
    return pl.pallas_call(
        _ffn_kernel,
        grid=(n // tm,),
        in_specs=[_row_spec(tm, d), prev, _const_spec((1, d)), vec, vec, vec, _const_spec((1, d)),
                  _layer_spec(w_in, layer), _const_spec(conv_w.shape), _const_spec((1, 2 * f)),
                  _layer_spec(w_out, layer)],
        out_specs=out_spec,
        out_shape=jax.ShapeDtypeStruct(out_shape, F32),
        scratch_shapes=[pltpu.VMEM((tm + halo, d), BF16),
                        pltpu.VMEM((tm, d), F32),
                        [pltpu.VMEM((tm + halo, 2 * FFN_TF), F32)] * 2,
                        pltpu.VMEM((tm, 2 * FFN_TF), BF16)] + relayout,
        compiler_params=_params("parallel"),
        name="conv_ffn",
    )(x, x, g2, sh, sc, gate, g3, w_in, conv_w, conv_b.reshape(1, 2 * f), w_out)


def kernel(x, c, positions, ada_w, ada_b, norm_g, s5_a_re, s5_a_im, s5_log_dt, s5_b_re, s5_b_im,
           s5_c_re, s5_c_im, s5_d, s5_w_glu, da_w_qkv, da_w_o, da_lq1, da_lk1, da_lq2, da_lk2,
           da_subln_g, ffn_w_in, ffn_conv_w, ffn_conv_b, ffn_w_out):
    depth = ada_w.shape[0]
    bsz, seq, d = x.shape
    n = seq * bsz
    mod = _ada_mod(c, ada_w, ada_b).reshape(depth, bsz, 6, d)
    lb_re, lb_im, bb_re, bb_im = _s5_discretise(s5_a_re, s5_a_im, s5_log_dt, s5_b_re, s5_b_im)
    cos, sin = _rope_tables(positions.T.reshape(n, 1))
    w_glu, w_qkv, w_o, w_in, w_out = (
        w.astype(BF16) for w in (s5_w_glu, da_w_qkv, da_w_o, ffn_w_in, ffn_w_out))
    for i in range(depth):
        sh_t, sc_t, g_t, sh_c, sc_c, g_c = (mod[i, :, m] for m in range(6))
        gains = norm_g[i].reshape(4, 1, d)
        j = i // 2
        if i % 2 == 0:
            packed = _s5_pack_weights(lb_re[j], lb_im[j], bb_re[j], bb_im[j], s5_c_re[j], s5_c_im[j])
            x = _s5_layer(x, gains[0], sh_t, sc_t, g_t, gains[1], packed, s5_d[j], w_glu, j)
        else:
            lambda_init = 0.8 - 0.6 * math.exp(-0.3 * i)
            q, k, v = _qkv_proj(x, gains[0], sh_t, sc_t, w_qkv, j, cos, sin)
            o = _diff_attention(q, k, v, da_lq1[j], da_lk1[j], da_lq2[j], da_lk2[j],
                                da_subln_g[j], lambda_init)
            x = _out_proj(o, x, w_o, j, gains[1], g_t)
        x = _conv_ffn(x, gains[2], sh_c, sc_c, g_c, gains[3], w_in, ffn_conv_w[i], ffn_conv_b[i],
                      w_out, i, batch_major_out=(i == depth - 1))
    return x
```

```python
import functools
import math

import jax
import jax.numpy as jnp
from jax import lax
from jax.experimental import pallas as pl
from jax.experimental.pallas import tpu as pltpu

F32 = jnp.float32
BF16 = jnp.bfloat16

EPS = 1e-6
CHUNK = 64
S5_GROUP = 16
S5_STATE = 64
S5_LAMBDA_RE_MAX = -1e-4
DA_HEADS = 8
DA_HEAD_DIM = 64
DA_V_DIM = 2 * DA_HEAD_DIM
ROPE_THETA = 10000.0
CONV_WIDTH = 3

LANES = 128
MXU_DIM = 256
VMEM_LIMIT_BYTES = 56 * 1024 * 1024

S5_PACK_GROUPS = MXU_DIM // S5_GROUP
S5_PACK_STATES = S5_PACK_GROUPS * S5_STATE
S5_SCAN_LANES = 4 * LANES
S5_FRAMES = 128

ROW_TILE = 1024
ATTN_TQ = 512
ATTN_TK = 1024
ATTN_DIAG_KEYS = 256
ATTN_SM_ROWS = 256
FFN_TF = 256
NEG_BIG = -1e30


def _params(*sem):
    return pltpu.CompilerParams(dimension_semantics=sem, vmem_limit_bytes=VMEM_LIMIT_BYTES)


def _const_spec(shape):
    nd = len(shape)
    return pl.BlockSpec(shape, lambda *_: (0,) * nd, pipeline_mode=pl.Buffered(1))


def _layer_spec(stack, layer):
    nd = stack.ndim
    return pl.BlockSpec((None,) + stack.shape[1:], lambda *_: (layer,) + (0,) * (nd - 1),
                        pipeline_mode=pl.Buffered(1))


def _row_spec(tm, width):
    return pl.BlockSpec((tm, width), lambda i: (i, 0))


def _rms(x, g):
    ms = jnp.mean(x * x, axis=-1, keepdims=True)
    return x * lax.rsqrt(ms + EPS) * g


def _per_seq(x, bsz, fn):
    rows, d = x.shape
    if bsz == 1:
        return fn(x)
    return fn(x.reshape(rows // bsz, bsz, d)).reshape(rows, d)


def _modulated_norm(x, g, sh, sc):
    return _per_seq(_rms(x, g), sh.shape[0], lambda y: y * (1.0 + sc) + sh)


def _gated_residual(x, y, g, gate):
    return x + _per_seq(_rms(y, g), gate.shape[0], lambda r: gate * r)


def _dot(a, b):
    return jnp.dot(a, b, preferred_element_type=F32)


def _relayout_scratch(rows, d):
    return [pltpu.VMEM((rows, LANES), F32)] * (d // LANES)


def _put_batch_major(dst_ref, j, val, tmp_ref):
    bsz, tt, _ = dst_ref.shape
    tmp_ref[...] = val
    for b in range(bsz):
        dst_ref[b, :, j * LANES:(j + 1) * LANES] = (
            tmp_ref[pl.ds(b, tt, stride=bsz), :].astype(dst_ref.dtype))


def _get_time_major(src_ref, tmp_refs):
    bsz, tt, _ = src_ref.shape
    for j, tmp_ref in enumerate(tmp_refs):
        for b in range(bsz):
            tmp_ref[pl.ds(b, tt, stride=bsz), :] = src_ref[b, :, j * LANES:(j + 1) * LANES].astype(F32)
    return jnp.concatenate([t[...] for t in tmp_refs], axis=1)


def _ada_kernel(c_ref, w_ref, b_ref, o_ref):
    c = c_ref[...]
    cond = (c * jax.nn.sigmoid(c)).astype(BF16)
    o_ref[...] = _dot(cond, w_ref[...].astype(BF16)) + b_ref[...]


def _ada_mod(c, ada_w, ada_b):
    depth, d, n = ada_w.shape
    bsz = c.shape[0]
    tn = n // 4
    return pl.pallas_call(
        _ada_kernel,
        grid=(depth, n // tn),
        in_specs=[
            pl.BlockSpec((bsz, d), lambda i, j: (0, 0)),
            pl.BlockSpec((None, d, tn), lambda i, j: (i, 0, j)),
            pl.BlockSpec((None, 1, tn), lambda i, j: (i, 0, j)),
        ],
        out_specs=pl.BlockSpec((None, bsz, tn), lambda i, j: (i, 0, j)),
        out_shape=jax.ShapeDtypeStruct((depth, bsz, n), F32),
        compiler_params=_params("parallel", "parallel"),
        name="ada_mod",
    )(c, ada_w, ada_b.reshape(depth, 1, n))


def _s5_disc_kernel(are_ref, aim_ref, ldt_ref, br_ref, bi_ref, lbr_ref, lbi_ref, bbr_ref, bbi_ref):
    lam_re = jnp.minimum(are_ref[...], S5_LAMBDA_RE_MAX)
    lam_im = aim_ref[...]
    dt = jnp.exp(ldt_ref[...])
    dre, dimg = lam_re * dt, lam_im * dt
    mag = jnp.exp(dre)
    lb_re, lb_im = mag * jnp.cos(dimg), mag * jnp.sin(dimg)
    den = lam_re * lam_re + lam_im * lam_im
    nr = lb_re - 1.0
    f_re = (nr * lam_re + lb_im * lam_im) / den
    f_im = (lb_im * lam_re - nr * lam_im) / den
    br, bi = br_ref[...], bi_ref[...]
    lbr_ref[...] = lb_re
    lbi_ref[...] = lb_im
    bbr_ref[...] = f_re * br - f_im * bi
    bbi_ref[...] = f_re * bi + f_im * br


def _s5_discretise(a_re, a_im, log_dt, b_re, b_im):
    n, g, p = a_re.shape
    cg = b_re.shape[-1]
    vec = pl.BlockSpec((None, g, 1, p), lambda i: (i, 0, 0, 0))
    mat = pl.BlockSpec((None, g, cg, p), lambda i: (i, 0, 0, 0))
    return pl.pallas_call(
        _s5_disc_kernel,
        grid=(n,),
        in_specs=[vec, vec, pl.BlockSpec((None, g, 1, 1), lambda i: (i, 0, 0, 0)), mat, mat],
        out_specs=[vec, vec, mat, mat],
        out_shape=[jax.ShapeDtypeStruct((n, g, 1, p), F32)] * 2
        + [jax.ShapeDtypeStruct((n, g, cg, p), F32)] * 2,
        compiler_params=_params("parallel"),
        name="s5_discretise",
    )(a_re.reshape(n, g, 1, p), a_im.reshape(n, g, 1, p), log_dt.reshape(n, g, 1, 1),
      jnp.swapaxes(b_re, -1, -2), jnp.swapaxes(b_im, -1, -2))


def _s5_pack_weights(lb_re, lb_im, bb_re, bb_im, c_re, c_im):
    g, cg, p = bb_re.shape
    gp = S5_PACK_GROUPS
    npack = g // gp

    def block_diag(m, rows_per_group, cols_per_group):
        tiled = jnp.tile(m, (1, 1, gp))
        r = lax.broadcasted_iota(jnp.int32, tiled.shape[1:], 0) // rows_per_group
        c = lax.broadcasted_iota(jnp.int32, tiled.shape[1:], 1) // cols_per_group
        return jnp.where(r == c, tiled, 0.0)

    def in_proj(bb):
        return block_diag(bb.reshape(npack, gp * cg, p), cg, p)

    def out_proj(c):
        return block_diag(jnp.swapaxes(c, -1, -2).reshape(npack, gp * p, cg), p, cg)

    wb = jnp.concatenate([in_proj(bb_re), in_proj(bb_im)], axis=-1).astype(BF16)
    return (wb, out_proj(c_re).astype(BF16), out_proj(c_im).astype(BF16),
            lb_re.reshape(npack, 1, gp * p), lb_im.reshape(npack, 1, gp * p))


def _s5_kernel(x_ref, g0_ref, sh_ref, sc_ref, gate_ref, g1_ref, wb_ref, lbr_ref, lbi_ref,
               wcr_ref, wci_ref, dsk_ref, wglu_ref, out_ref, h_ref, u_ref, bu_refs, y_ref,
               *relayout):
    if relayout:
        xtm_ref, *tmp_refs = relayout
        xtm_ref[...] = _get_time_major(x_ref, tmp_refs)
        x_ref = xtm_ref
    rows, d = x_ref.shape
    bsz = sh_ref.shape[0]
    npack = wb_ref.shape[0]
    ns = S5_PACK_STATES

    @pl.when(pl.program_id(0) == 0)
    def _():
        h_ref[...] = jnp.zeros_like(h_ref)

    u_ref[...] = _modulated_norm(x_ref[...], g0_ref[...], sh_ref[...], sc_ref[...])

    for k in range(npack):
        c0 = k * MXU_DIM
        bu_ref = bu_refs[k % 2]
        bu_ref[...] = _dot(u_ref[:, c0:c0 + MXU_DIM].astype(BF16), wb_ref[k])
        for s in range(ns // S5_SCAN_LANES):
            re = pl.ds(s * S5_SCAN_LANES, S5_SCAN_LANES)
            im = pl.ds(ns + s * S5_SCAN_LANES, S5_SCAN_LANES)
            lbr = jnp.broadcast_to(lbr_ref[k, :, re], (bsz, S5_SCAN_LANES))
            lbi = jnp.broadcast_to(lbi_ref[k, :, re], (bsz, S5_SCAN_LANES))
            hr, hi = h_ref[k, :, re], h_ref[k, :, im]
            for t in range(rows // bsz):
                frame = pl.ds(t * bsz, bsz)
                hr, hi = (lbr * hr - lbi * hi + bu_ref[frame, re],
                          lbr * hi + lbi * hr + bu_ref[frame, im])
                bu_ref[frame, re] = hr
                bu_ref[frame, im] = hi
            h_ref[k, :, re] = hr
            h_ref[k, :, im] = hi
        y_ref[:, c0:c0 + MXU_DIM] = (_dot(bu_ref[:, :ns].astype(BF16), wcr_ref[k])
                                     - _dot(bu_ref[:, ns:].astype(BF16), wci_ref[k]))

    y = y_ref[...] + dsk_ref[...] * u_ref[...]
    o = _dot(jax.nn.gelu(y).astype(BF16), wglu_ref[...])
    mix = o[:, :d] * jax.nn.sigmoid(o[:, d:])
    out_ref[...] = _gated_residual(x_ref[...], mix, g1_ref[...], gate_ref[...])


def _s5_layer(x, g0, sh, sc, gate, g1, packed, d_skip, w_glu, layer):
    bsz, d = sh.shape
    n = x.size // d
    wb, wcr, wci, lbr, lbi = packed
    npack = wb.shape[0]
    rows = S5_FRAMES * bsz
    if x.ndim == 3:
        x_spec = pl.BlockSpec((bsz, S5_FRAMES, d), lambda t: (0, t, 0))
        relayout = [pltpu.VMEM((rows, d), F32)] + _relayout_scratch(rows, d)
    else:
        x_spec, relayout = _row_spec(rows, d), []
    return pl.pallas_call(
        _s5_kernel,
        grid=(n // rows,),
        in_specs=[x_spec, _const_spec((1, d)), _const_spec((bsz, d)),
                  _const_spec((bsz, d)), _const_spec((bsz, d)), _const_spec((1, d)),
                  _const_spec(wb.shape), _const_spec(lbr.shape), _const_spec(lbi.shape),
                  _const_spec(wcr.shape), _const_spec(wci.shape), _const_spec((1, d)),
                  _layer_spec(w_glu, layer)],
        out_specs=_row_spec(rows, d),
        out_shape=jax.ShapeDtypeStruct((n, d), F32),
        scratch_shapes=[pltpu.VMEM((npack, bsz, 2 * S5_PACK_STATES), F32),
                        pltpu.VMEM((rows, d), F32),
                        [pltpu.VMEM((rows, 2 * S5_PACK_STATES), F32)] * 2,
                        pltpu.VMEM((rows, d), F32)] + relayout,
        compiler_params=_params("arbitrary"),
        name="s5_layer",
    )(x, g0, sh, sc, gate, g1, wb, lbr, lbi, wcr, wci, d_skip.reshape(1, d), w_glu)


def _rope_kernel(pos_ref, inv_ref, cos_ref, sin_ref):
    ang = pos_ref[...].astype(F32) * inv_ref[...]
    lane = lax.broadcasted_iota(jnp.int32, ang.shape, 1)
    first_half = lane % DA_HEAD_DIM < DA_HEAD_DIM // 2
    cos_ref[...] = jnp.cos(ang)
    sin_ref[...] = jnp.where(first_half, -1.0, 1.0) * jnp.sin(ang)


def _rope_tables(pos_rows):
    n = pos_rows.shape[0]
    half = DA_HEAD_DIM // 2
    inv = ROPE_THETA ** (-jnp.arange(half, dtype=F32) / half)
    inv = jnp.tile(inv, LANES // half).reshape(1, LANES)
    tm = min(n, ROW_TILE)
    return pl.pallas_call(
        _rope_kernel,
        grid=(n // tm,),
        in_specs=[_row_spec(tm, 1), pl.BlockSpec((1, LANES), lambda i: (0, 0))],
        out_specs=[_row_spec(tm, LANES)] * 2,
        out_shape=[jax.ShapeDtypeStruct((n, LANES), F32)] * 2,
        compiler_params=_params("parallel"),
        name="rope_tables",
    )(pos_rows, inv)


def _qkv_kernel(x_ref, g_ref, sh_ref, sc_ref, w_ref, cos_ref, sin_ref, q_ref, k_ref, v_ref,
                *tmp_refs):
    d = x_ref.shape[-1]
    h = _modulated_norm(x_ref[...], g_ref[...], sh_ref[...], sc_ref[...]).astype(BF16)
    cos, sin = cos_ref[...], sin_ref[...]
    lane = lax.broadcasted_iota(jnp.int32, cos.shape, 1)
    half = DA_HEAD_DIM // 2
    first_half = lane % DA_HEAD_DIM < half

    for src, dst, scale in ((0, q_ref, DA_HEAD_DIM ** -0.5), (d, k_ref, 1.0)):
        t = _dot(h, w_ref[:, src:src + d])
        for j in range(d // LANES):
            tj = t[:, j * LANES:(j + 1) * LANES]
            rot = jnp.where(first_half, pltpu.roll(tj, LANES - half, 1), pltpu.roll(tj, half, 1))
            _put_batch_major(dst, j, (tj * cos + rot * sin) * scale, tmp_refs[j])
    t = _dot(h, w_ref[:, 2 * d:])
    for j in range(d // LANES):
        _put_batch_major(v_ref, j, t[:, j * LANES:(j + 1) * LANES], tmp_refs[j])


def _qkv_proj(x, g, sh, sc, w_qkv, layer, cos, sin):
    n, d = x.shape
    bsz = sh.shape[0]
    tm = min(n, ROW_TILE)
    batch_major = pl.BlockSpec((bsz, tm // bsz, d), lambda i: (0, i, 0))
    return pl.pallas_call(
        _qkv_kernel,
        grid=(n // tm,),
        in_specs=[_row_spec(tm, d), _const_spec((1, d)), _const_spec((bsz, d)),
                  _const_spec((bsz, d)), _layer_spec(w_qkv, layer), _row_spec(tm, LANES),
                  _row_spec(tm, LANES)],
        out_specs=[batch_major] * 3,
        out_shape=[jax.ShapeDtypeStruct((bsz, n // bsz, d), BF16)] * 3,
        scratch_shapes=_relayout_scratch(tm, d),
        compiler_params=_params("parallel"),
        name="qkv_proj",
    )(x, g, sh, sc, w_qkv, cos, sin)


def _attn_kernel(q_ref, k_ref, v_ref, lq1_ref, lk1_ref, lq2_ref, lk2_ref, sg_ref, o_ref,
                 kt_ref, vext_ref, qs_ref, m_ref, acc_ref, s_refs, p_refs, a_refs, *,
                 lambda_init, tq):
    seq = q_ref.shape[0]
    vext_ref[:, :DA_V_DIM] = v_ref[...]
    col = lax.broadcasted_iota(jnp.int32, (seq, DA_V_DIM), 1)
    vext_ref[:, DA_V_DIM:] = jnp.where(col == 0, 1.0, 0.0).astype(BF16)
    tb = min(seq, ROW_TILE)
    for blk in range(seq // tb):
        kt_ref[:, blk * tb:(blk + 1) * tb] = (
            k_ref[blk * tb:(blk + 1) * tb, :].astype(F32).T.astype(BF16))

    lam = (jnp.exp(jnp.sum(lq1_ref[...] * lk1_ref[...], keepdims=True))
           - jnp.exp(jnp.sum(lq2_ref[...] * lk2_ref[...], keepdims=True)) + lambda_init)

    lane = lax.broadcasted_iota(jnp.int32, (tq, 2 * DA_HEAD_DIM), 1)
    first = lane < DA_HEAD_DIM
    rows = 2 * tq
    tk = min(seq, ATTN_TK)

    def row_ranges(blk):
        lo = blk[2] or 0
        return [(comp * tq + lo, (comp + 1) * tq) for comp in range(2)] if lo else [(0, rows)]

    def scores(t, blk):
        k0, kw, _ = blk
        kt = kt_ref[:, k0:k0 + kw]
        for r0, r1 in row_ranges(blk):
            s_refs[t % 2][r0:r1, :kw] = _dot(qs_ref[r0:r1, :], kt)

    def softmax(t, blk, first):
        _, kw, diag = blk
        for r0, r1 in row_ranges(blk):
            for i0 in range(r0, r1, ATTN_SM_ROWS):
                rs = pl.ds(i0, ATTN_SM_ROWS)
                s = s_refs[t % 2][rs, :kw]
                if diag is not None:
                    qrow = i0 % tq + lax.broadcasted_iota(jnp.int32, s.shape, 0)
                    key = diag + lax.broadcasted_iota(jnp.int32, s.shape, 1)
                    s = jnp.where(key < (qrow // CHUNK + 1) * CHUNK, s, NEG_BIG)
                parts = [s[:, j * LANES:(j + 1) * LANES] for j in range(kw // LANES)]
                m_new = jnp.max(functools.reduce(jnp.maximum, parts), axis=-1, keepdims=True)
                if first:
                    m_new = jnp.broadcast_to(m_new, (ATTN_SM_ROWS, LANES))
                else:
                    m_prev = m_ref[rs, :]
                    m_new = jnp.maximum(m_prev, m_new)
                    a_refs[t % 2][rs, :] = jnp.exp(m_prev - m_new)
                p_refs[t % 2][rs, :kw] = jnp.concatenate(
                    [jnp.exp(pj - m_new) for pj in parts], axis=1).astype(BF16)
                m_ref[rs, :] = m_new

    def values(t, blk, first):
        k0, kw, _ = blk
        vb = vext_ref[k0:k0 + kw, :]
        for r0, r1 in row_ranges(blk):
            pv = _dot(p_refs[t % 2][r0:r1, :kw], vb)
            if not first:
                alpha = a_refs[t % 2][r0:r1, :]
                pv = jnp.concatenate([alpha, alpha], axis=1) * acc_ref[r0:r1, :] + pv
            acc_ref[r0:r1, :] = pv

    for qi in range(seq // tq):
        q0 = qi * tq
        qb = q_ref[q0:q0 + tq, :]
        zero = jnp.zeros_like(qb)
        qs_ref[:tq, :] = jnp.where(first, qb, zero)
        qs_ref[tq:, :] = jnp.where(first, zero, qb)

        blocks = [(k0, min(tk, q0 - k0), None) for k0 in range(0, q0, tk)]
        blocks += [(q0 + off, ATTN_DIAG_KEYS, off) for off in range(0, tq, ATTN_DIAG_KEYS)]
        scores(0, blocks[0])
        for t, blk in enumerate(blocks):
            if t + 1 < len(blocks):
                scores(t + 1, blocks[t + 1])
            softmax(t, blk, t == 0)
            if t >= 1:
                values(t - 1, blocks[t - 1], t == 1)
        values(len(blocks) - 1, blocks[-1], False)

        for i in range(tq // ATTN_SM_ROWS):
            o = []
            for comp in range(2):
                r0 = comp * tq + i * ATTN_SM_ROWS
                acc = acc_ref[r0:r0 + ATTN_SM_ROWS, :]
                o.append(acc[:, :DA_V_DIM] / acc[:, DA_V_DIM:DA_V_DIM + 1])
            od = _rms(o[0] - lam * o[1], sg_ref[...]) * (1.0 - lambda_init)
            r0 = q0 + i * ATTN_SM_ROWS
            o_ref[r0:r0 + ATTN_SM_ROWS, :] = od.astype(BF16)


def _diff_attention(q, k, v, lq1, lk1, lq2, lk2, subln_g, lambda_init):
    bsz, seq, d = q.shape
    tq = min(seq, ATTN_TQ)
    rows, tk = 2 * tq, min(seq, ATTN_TK)
    head = pl.BlockSpec((None, seq, DA_V_DIM), lambda b, h: (b, 0, h))
    vec = _const_spec((1, DA_HEAD_DIM))
    return pl.pallas_call(
        functools.partial(_attn_kernel, lambda_init=lambda_init, tq=tq),
        grid=(bsz, d // DA_V_DIM),
        in_specs=[head, head, head, vec, vec, vec, vec, _const_spec((1, DA_V_DIM))],
        out_specs=head,
        out_shape=jax.ShapeDtypeStruct((bsz, seq, d), BF16),
        scratch_shapes=[pltpu.VMEM((2 * DA_HEAD_DIM, seq), BF16),
                        pltpu.VMEM((seq, 2 * DA_V_DIM), BF16),
                        pltpu.VMEM((rows, 2 * DA_HEAD_DIM), BF16),
                        pltpu.VMEM((rows, LANES), F32),
                        pltpu.VMEM((rows, 2 * DA_V_DIM), F32),
                        [pltpu.VMEM((rows, tk), F32)] * 2,
                        [pltpu.VMEM((rows, tk), BF16)] * 2,
                        [pltpu.VMEM((rows, LANES), F32)] * 2],
        compiler_params=_params("parallel", "parallel"),
        name="diff_attention",
    )(q, k, v, lq1.reshape(1, -1), lk1.reshape(1, -1), lq2.reshape(1, -1), lk2.reshape(1, -1),
      subln_g.reshape(1, -1))


def _oproj_kernel(o_ref, x_ref, w_ref, g_ref, gate_ref, out_ref, *tmp_refs):
    o = _get_time_major(o_ref, tmp_refs).astype(BF16)
    out_ref[...] = _gated_residual(x_ref[...], _dot(o, w_ref[...]), g_ref[...], gate_ref[...])


def _out_proj(o, x, w_o, layer, g, gate):
    n, d = x.shape
    bsz = gate.shape[0]
    tm = min(n, ROW_TILE)
    return pl.pallas_call(
        _oproj_kernel,
        grid=(n // tm,),
        in_specs=[pl.BlockSpec((bsz, tm // bsz, d), lambda i: (0, i, 0)), _row_spec(tm, d),
                  _layer_spec(w_o, layer), _const_spec((1, d)), _const_spec((bsz, d))],
        out_specs=_row_spec(tm, d),
        out_shape=jax.ShapeDtypeStruct(x.shape, F32),
        scratch_shapes=_relayout_scratch(tm, d),
        compiler_params=_params("parallel"),
        name="attn_out_proj",
    )(o, x, w_o, g, gate)


def _ffn_kernel(x_ref, xprev_ref, g2_ref, sh_ref, sc_ref, gate_ref, g3_ref, win_ref, cw_ref, cb_ref,
                wout_ref, out_ref, h_ref, acc_ref, u_refs, g_ref, *tmp_refs):
    tm, d = x_ref.shape
    f = wout_ref.shape[0]
    bsz = sh_ref.shape[0]
    halo = xprev_ref.shape[0]
    tf = FFN_TF
    n_tiles = f // tf
    g2, sh, sc = g2_ref[...], sh_ref[...], sc_ref[...]
    h_ref[:halo, :] = _modulated_norm(xprev_ref[...], g2, sh, sc).astype(BF16)
    h_ref[halo:, :] = _modulated_norm(x_ref[...], g2, sh, sc).astype(BF16)
    keep_halo = jnp.where(pl.program_id(0) > 0, 1.0, 0.0)
    acc_ref[...] = jnp.zeros_like(acc_ref)

    def up(i, slot):
        c0 = i * tf
        hb = h_ref[...]
        for half, col in enumerate((pl.ds(c0, tf), pl.ds(f + c0, tf))):
            u_refs[slot][:, half * tf:(half + 1) * tf] = _dot(hb, win_ref[:, col])
        u_refs[slot][:halo, :] = u_refs[slot][:halo, :] * keep_halo

    def gate(i, slot):
        c0 = i * tf
        halves = []
        for half, col in enumerate((pl.ds(c0, tf), pl.ds(f + c0, tf))):
            w = cw_ref[:, col].astype(BF16)
            out = cb_ref[:, col].astype(BF16)
            for j in range(CONV_WIDTH):
                u = u_refs[slot][pl.ds(j * bsz, tm), half * tf:(half + 1) * tf]
                out = out + w[j:j + 1] * u.astype(BF16)
            halves.append(out)
        g_ref[:, slot * tf:(slot + 1) * tf] = jax.nn.gelu(halves[0]) * halves[1]

    def down(i, ntile):
        c0 = i * tf
        acc_ref[...] += _dot(g_ref[:, :ntile * tf], wout_ref[pl.ds(c0, ntile * tf), :])

    assert n_tiles % 2 == 1
    up(0, 0)
    for i in range(0, n_tiles - 1, 2):
        up(i + 1, 1)
        gate(i, 0)
        up(i + 2, 0)
        gate(i + 1, 1)
        down(i, 2)
    gate(n_tiles - 1, 0)
    down(n_tiles - 1, 1)
    res = _gated_residual(x_ref[...], acc_ref[...], g3_ref[...], gate_ref[...])
    if tmp_refs:
        for j, tmp_ref in enumerate(tmp_refs):
            _put_batch_major(out_ref, j, res[:, j * LANES:(j + 1) * LANES], tmp_ref)
    else:
        out_ref[...] = res


def _conv_ffn(x, g2, sh, sc, gate, g3, w_in, conv_w, conv_b, w_out, layer, batch_major_out=False):
    n, d = x.shape
    f = w_out.shape[1]
    bsz = sh.shape[0]
    tm = min(n, ROW_TILE)
    halo = (CONV_WIDTH - 1) * bsz
    prev = pl.BlockSpec((halo, d), lambda i: (jnp.maximum(i * (tm // halo) - 1, 0), 0))
    vec = _const_spec((bsz, d))
    if batch_major_out:
        out_spec = pl.BlockSpec((bsz, tm // bsz, d), lambda i: (0, i, 0))
        out_shape, relayout = (bsz, n // bsz, d), _relayout_scratch(tm, d)
    else:
        out_spec, out_shape, relayout = _row_spec(tm, d), (n, d), []
    return pl.pallas_call(
        _ffn_kernel,
        grid=(n // tm,),
        in_specs=[_row_spec(tm, d), prev, _const_spec((1, d)), vec, vec, vec, _const_spec((1, d)),
                  _layer_spec(w_in, layer), _const_spec(conv_w.shape), _const_spec((1, 2 * f)),
                  _layer_spec(w_out, layer)],
        out_specs=out_spec,
        out_shape=jax.ShapeDtypeStruct(out_shape, F32),
        scratch_shapes=[pltpu.VMEM((tm + halo, d), BF16),
                        pltpu.VMEM((tm, d), F32),
                        [pltpu.VMEM((tm + halo, 2 * FFN_TF), F32)] * 2,
                        pltpu.VMEM((tm, 2 * FFN_TF), BF16)] + relayout,
        compiler_params=_params("parallel"),
        name="conv_ffn",
    )(x, x, g2, sh, sc, gate, g3, w_in, conv_w, conv_b.reshape(1, 2 * f), w_out)


def kernel(x, c, positions, ada_w, ada_b, norm_g, s5_a_re, s5_a_im, s5_log_dt, s5_b_re, s5_b_im,
           s5_c_re, s5_c_im, s5_d, s5_w_glu, da_w_qkv, da_w_o, da_lq1, da_lk1, da_lq2, da_lk2,
           da_subln_g, ffn_w_in, ffn_conv_w, ffn_conv_b, ffn_w_out):
    depth = ada_w.shape[0]
    bsz, seq, d = x.shape
    n = seq * bsz
    mod = _ada_mod(c, ada_w, ada_b).reshape(depth, bsz, 6, d)
    lb_re, lb_im, bb_re, bb_im = _s5_discretise(s5_a_re, s5_a_im, s5_log_dt, s5_b_re, s5_b_im)
    cos, sin = _rope_tables(positions.T.reshape(n, 1))
    w_glu, w_qkv, w_o, w_in, w_out = (
        w.astype(BF16) for w in (s5_w_glu, da_w_qkv, da_w_o, ffn_w_in, ffn_w_out))
    for i in range(depth):
        sh_t, sc_t, g_t, sh_c, sc_c, g_c = (mod[i, :, m] for m in range(6))
        gains = norm_g[i].reshape(4, 1, d)
        j = i // 2
        if i % 2 == 0:
            packed = _s5_pack_weights(lb_re[j], lb_im[j], bb_re[j], bb_im[j], s5_c_re[j], s5_c_im[j])
            x = _s5_layer(x, gains[0], sh_t, sc_t, g_t, gains[1], packed, s5_d[j], w_glu, j)
        else:
            lambda_init = 0.8 - 0.6 * math.exp(-0.3 * i)
            q, k, v = _qkv_proj(x, gains[0], sh_t, sc_t, w_qkv, j, cos, sin)
            o = _diff_attention(q, k, v, da_lq1[j], da_lk1[j], da_lq2[j], da_lk2[j],
                                da_subln_g[j], lambda_init)
            x = _out_proj(o, x, w_o, j, gains[1], g_t)
        x = _conv_ffn(x, gains[2], sh_c, sc_c, g_c, gains[3], w_in, ffn_conv_w[i], ffn_conv_b[i],
                      w_out, i, batch_major_out=(i == depth - 1))
    return x
```
